```python
import jax, jax.numpy as jnp
from jax import lax
import numpy as np

D_MODEL = 1024
BATCH = 8
SEQ = 2048
DEPTH = 4
DEC_BATCH = 128
DEC_SEQ = 4
PAST_LEN = 2048
PAGE_SIZE = 128

BRANCH_WIDTH = D_MODEL
LRU_WIDTH = BRANCH_WIDTH
LRU_HEADS = 8
LRU_BLOCK = LRU_WIDTH // LRU_HEADS
LRU_C = 8.0
CONV_A_WIDTH = 4
SC_WIDTH = BRANCH_WIDTH
CONV_B_WIDTH = 3
SB_HEADS = 8
SB_HEAD_DIM = BRANCH_WIDTH // SB_HEADS
SB_WIDTH = SB_HEADS * SB_HEAD_DIM
SB_BIAS_INIT = -6.0
Q_BLOCK = 128
N_BRANCH = 3
N_IN_PIECES = 8
IN_COLS = N_IN_PIECES * BRANCH_WIDTH
D_FF = 2816
RMS_EPS = 1e-6

kernel_name = "hawk_shortconv_stickbreak_macaron_step"


def rmsnorm(x, g):
    xf = x.astype(jnp.float32)
    y = xf * lax.rsqrt(jnp.mean(xf * xf, axis=-1, keepdims=True) + RMS_EPS)
    return (y * g.astype(jnp.float32)).astype(x.dtype)


def swiglu(x, w1, w3, w2):
    return (jax.nn.silu(x @ w1) * (x @ w3)) @ w2


def causal_dwconv(u, buf, w):
    k = w.shape[0]
    t = u.shape[1]
    full = jnp.concatenate([buf.astype(u.dtype), u], axis=1)
    y = full[:, 0:t] * w[0]
    for j in range(1, k):
        y = y + full[:, j:j + t] * w[j]
    return y, full[:, full.shape[1] - (k - 1):]


def linear_recurrence(a, b, h0):
    b = b.at[:, 0].add(a[:, 0] * h0)

    def combine(lhs, rhs):
        a1, b1 = lhs
        a2, b2 = rhs
        return a1 * a2, a2 * b1 + b2

    _, h = lax.associative_scan(combine, (a, b), axis=1)
    return h


def rg_lru(xc, h0, w_a, b_a, w_x, b_x, lam):
    bsz, t, _ = xc.shape
    f32 = jnp.float32
    xf = xc.astype(f32)
    xb = xf.reshape(bsz, t, LRU_HEADS, LRU_BLOCK)
    r = jax.nn.sigmoid(jnp.einsum('bthi,hij->bthj', xb, w_a.astype(f32)).reshape(bsz, t, LRU_WIDTH) + b_a.astype(f32))
    i = jax.nn.sigmoid(jnp.einsum('bthi,hij->bthj', xb, w_x.astype(f32)).reshape(bsz, t, LRU_WIDTH) + b_x.astype(f32))
    log_a = -LRU_C * jax.nn.softplus(-lam.astype(f32)) * r
    a = jnp.exp(log_a)
    mult = jnp.sqrt(-jnp.expm1(2.0 * log_a))
    h = linear_recurrence(a, mult * (i * xf), h0.astype(f32))
    return h.astype(xc.dtype), h[:, -1].astype(xc.dtype)


def stick_breaking(q, k, v, bias, q_pos, k_pos):
    f32 = jnp.float32
    z = jnp.einsum('bqhd,bshd->bhqs', q.astype(f32), k.astype(f32)) * (SB_HEAD_DIM ** -0.5)
    z = z + bias.astype(f32)[None, :, None, None]
    valid = k_pos[None, :] < q_pos[:, None]
    log_beta = jax.nn.log_sigmoid(z)
    log_1mb = jnp.where(valid, log_beta - z, 0.0)
    between = lax.cumsum(log_1mb, axis=3, reverse=True) - log_1mb
    w = jnp.where(valid, jnp.exp(log_beta + between), 0.0)
    return jnp.einsum('bhqs,bshd->bqhd', w, v.astype(f32)).astype(v.dtype)


def stick_breaking_prompt(q, k, v, bias):
    bsz, t, h, d = q.shape
    nb = t // Q_BLOCK
    qb = q.reshape(bsz, nb, Q_BLOCK, h, d).transpose(1, 0, 2, 3, 4)
    pos = jnp.arange(t)
    qpos = pos.reshape(nb, Q_BLOCK)
    out = lax.map(lambda blk: stick_breaking(blk[0], k, v, bias, blk[1], pos), (qb, qpos))
    return out.transpose(1, 0, 2, 3, 4).reshape(bsz, t, h, d)


def mixer(xn, lp, conv_a_buf, h0, conv_b_buf, k_past, v_past):
    bsz, t, _ = xn.shape
    proj = xn @ lp['w_in']
    a_x, a_g, sc_b, sc_c, sc_h, q, k, v = jnp.split(proj, N_IN_PIECES, axis=-1)
    xc, conv_a_new = causal_dwconv(a_x, conv_a_buf, lp['conv_a_w'])
    xc = xc + lp['conv_a_b']
    hseq, h_last = rg_lru(xc, h0, lp['rg_w_a'], lp['rg_b_a'], lp['rg_w_x'], lp['rg_b_x'], lp['rg_lambda'])
    y_a = hseq * jax.nn.gelu(a_g)
    cu, conv_b_new = causal_dwconv(sc_c * sc_h, conv_b_buf, lp['conv_b_w'])
    y_b = sc_b * cu
    q = q.reshape(bsz, t, SB_HEADS, SB_HEAD_DIM)
    k = k.reshape(bsz, t, SB_HEADS, SB_HEAD_DIM)
    v = v.reshape(bsz, t, SB_HEADS, SB_HEAD_DIM)
    if k_past is None:
        y_c = stick_breaking_prompt(q, k, v, lp['sb_bias'])
    else:
        past = k_past.shape[1]
        y_c = stick_breaking(q, jnp.concatenate([k_past.astype(k.dtype), k], axis=1),
                             jnp.concatenate([v_past.astype(v.dtype), v], axis=1),
                             lp['sb_bias'], past + jnp.arange(t), jnp.arange(past + t))
    y_c = y_c.reshape(bsz, t, SB_WIDTH)
    branches = jnp.stack([y_a, y_b, y_c], axis=2)
    proj_br = jnp.einsum('btgc,gcd->btgd', branches, lp['w_branch'])
    gates = jax.nn.sigmoid(xn @ lp['w_gate'] + lp['b_gate']).reshape(bsz, t, N_BRANCH, D_MODEL)
    out = jnp.sum(gates * proj_br, axis=2) @ lp['w_o']
    return out, (k, v, h_last, conv_a_new, conv_b_new)


def layer(x, lp, conv_a_buf, h0, conv_b_buf, k_past, v_past):
    x = x + 0.5 * swiglu(rmsnorm(x, lp['norm_ffn1']), lp['ffn1_w1'], lp['ffn1_w3'], lp['ffn1_w2'])
    m, st = mixer(rmsnorm(x, lp['norm_mix']), lp, conv_a_buf, h0, conv_b_buf, k_past, v_past)
    x = x + m
    x = x + 0.5 * swiglu(rmsnorm(x, lp['norm_ffn2']), lp['ffn2_w1'], lp['ffn2_w3'], lp['ffn2_w2'])
    return x, st


def setup_inputs(seed: int = 0) -> dict:
    key = jax.random.key(seed)
    ks = jax.random.split(key, 40)
    f32 = jnp.float32
    n_pages = PAST_LEN // PAGE_SIZE
    n_phys = (DEC_BATCH * n_pages * 5) // 4

    def nrm(k, shape, scale):
        return jax.random.normal(k, shape, f32) * scale

    def gain(k, shape):
        return 1.0 + 0.01 * jax.random.normal(k, shape, f32)

    a8 = jax.random.uniform(ks[0], (DEPTH, LRU_WIDTH), f32, minval=0.9, maxval=0.999)
    a_base = a8 ** (1.0 / LRU_C)
    rg_lambda = jnp.log(a_base) - jnp.log1p(-a_base)
    page_table = jax.random.permutation(ks[1], n_phys)[:DEC_BATCH * n_pages].reshape(DEC_BATCH, n_pages).astype(jnp.int32)
    return {
        'x_prompt': nrm(ks[2], (BATCH, SEQ, D_MODEL), 1.0),
        'x_sample': nrm(ks[3], (DEC_BATCH, DEC_SEQ, D_MODEL), 1.0),
        'cache_k': nrm(ks[4], (DEPTH, n_phys, PAGE_SIZE, SB_HEADS, SB_HEAD_DIM), 1.0),
        'cache_v': nrm(ks[5], (DEPTH, n_phys, PAGE_SIZE, SB_HEADS, SB_HEAD_DIM), 1.0),
        'page_table': page_table,
        'state_lru_h': nrm(ks[6], (DEPTH, DEC_BATCH, LRU_WIDTH), 0.5),
        'state_conv_a': nrm(ks[7], (DEPTH, DEC_BATCH, CONV_A_WIDTH - 1, LRU_WIDTH), 1.0),
        'state_conv_b': nrm(ks[8], (DEPTH, DEC_BATCH, CONV_B_WIDTH - 1, SC_WIDTH), 1.0),
        'norm_ffn1': gain(ks[9], (DEPTH, D_MODEL)),
        'ffn1_w1': nrm(ks[10], (DEPTH, D_MODEL, D_FF), D_MODEL ** -0.5),
        'ffn1_w3': nrm(ks[11], (DEPTH, D_MODEL, D_FF), D_MODEL ** -0.5),
        'ffn1_w2': nrm(ks[12], (DEPTH, D_FF, D_MODEL), D_FF ** -0.5),
        'norm_mix': gain(ks[13], (DEPTH, D_MODEL)),
        'w_in': nrm(ks[14], (DEPTH, D_MODEL, IN_COLS), D_MODEL ** -0.5),
        'conv_a_w': nrm(ks[15], (DEPTH, CONV_A_WIDTH, LRU_WIDTH), CONV_A_WIDTH ** -0.5),
        'conv_a_b': nrm(ks[16], (DEPTH, LRU_WIDTH), 0.01),
        'rg_w_a': nrm(ks[17], (DEPTH, LRU_HEADS, LRU_BLOCK, LRU_BLOCK), LRU_BLOCK ** -0.5),
        'rg_b_a': nrm(ks[18], (DEPTH, LRU_WIDTH), 0.01),
        'rg_w_x': nrm(ks[19], (DEPTH, LRU_HEADS, LRU_BLOCK, LRU_BLOCK), LRU_BLOCK ** -0.5),
        'rg_b_x': nrm(ks[20], (DEPTH, LRU_WIDTH), 0.01),
        'rg_lambda': rg_lambda,
        'conv_b_w': nrm(ks[21], (DEPTH, CONV_B_WIDTH, SC_WIDTH), CONV_B_WIDTH ** -0.5),
        'sb_bias': SB_BIAS_INIT + nrm(ks[31], (DEPTH, SB_HEADS), 0.1),
        'w_branch': nrm(ks[22], (DEPTH, N_BRANCH, BRANCH_WIDTH, D_MODEL), BRANCH_WIDTH ** -0.5),
        'w_gate': nrm(ks[23], (DEPTH, D_MODEL, N_BRANCH * D_MODEL), D_MODEL ** -0.5),
        'b_gate': nrm(ks[24], (DEPTH, N_BRANCH * D_MODEL), 0.01),
        'w_o': nrm(ks[25], (DEPTH, D_MODEL, D_MODEL), D_MODEL ** -0.5),
        'norm_ffn2': gain(ks[26], (DEPTH, D_MODEL)),
        'ffn2_w1': nrm(ks[27], (DEPTH, D_MODEL, D_FF), D_MODEL ** -0.5),
        'ffn2_w3': nrm(ks[28], (DEPTH, D_MODEL, D_FF), D_MODEL ** -0.5),
        'ffn2_w2': nrm(ks[29], (DEPTH, D_FF, D_MODEL), D_FF ** -0.5),
        'norm_final': gain(ks[30], (D_MODEL,)),
    }


def reference(x_prompt, x_sample, cache_k, cache_v, page_table, state_lru_h, state_conv_a, state_conv_b,
              norm_ffn1, ffn1_w1, ffn1_w3, ffn1_w2, norm_mix, w_in, conv_a_w, conv_a_b,
              rg_w_a, rg_b_a, rg_w_x, rg_b_x, rg_lambda, conv_b_w, sb_bias, w_branch, w_gate, b_gate, w_o,
              norm_ffn2, ffn2_w1, ffn2_w3, ffn2_w2, norm_final):
    n_pages = page_table.shape[1]
    page = cache_k.shape[2]
    dt = x_prompt.dtype
    xp, xs = x_prompt, x_sample
    kp_l, vp_l, ks_l, vs_l = [], [], [], []
    hp_l, hs_l, cap_l, cas_l, cbp_l, cbs_l = [], [], [], [], [], []
    for l in range(DEPTH):
        lp = {
            'norm_ffn1': norm_ffn1[l], 'ffn1_w1': ffn1_w1[l], 'ffn1_w3': ffn1_w3[l], 'ffn1_w2': ffn1_w2[l],
            'norm_mix': norm_mix[l], 'w_in': w_in[l], 'conv_a_w': conv_a_w[l], 'conv_a_b': conv_a_b[l],
            'rg_w_a': rg_w_a[l], 'rg_b_a': rg_b_a[l], 'rg_w_x': rg_w_x[l], 'rg_b_x': rg_b_x[l],
            'rg_lambda': rg_lambda[l], 'conv_b_w': conv_b_w[l], 'sb_bias': sb_bias[l],
            'w_branch': w_branch[l], 'w_gate': w_gate[l], 'b_gate': b_gate[l], 'w_o': w_o[l],
            'norm_ffn2': norm_ffn2[l], 'ffn2_w1': ffn2_w1[l], 'ffn2_w3': ffn2_w3[l], 'ffn2_w2': ffn2_w2[l],
        }
        bp = xp.shape[0]
        xp, (k_n, v_n, h_n, ca_n, cb_n) = layer(
            xp, lp,
            jnp.zeros((bp, CONV_A_WIDTH - 1, LRU_WIDTH), dt),
            jnp.zeros((bp, LRU_WIDTH), dt),
            jnp.zeros((bp, CONV_B_WIDTH - 1, SC_WIDTH), dt),
            None, None)
        kp_l.append(k_n); vp_l.append(v_n); hp_l.append(h_n); cap_l.append(ca_n); cbp_l.append(cb_n)
        bs = xs.shape[0]
        k_past = cache_k[l][page_table].reshape(bs, n_pages * page, SB_HEADS, SB_HEAD_DIM)
        v_past = cache_v[l][page_table].reshape(bs, n_pages * page, SB_HEADS, SB_HEAD_DIM)
        xs, (k_n, v_n, h_n, ca_n, cb_n) = layer(
            xs, lp, state_conv_a[l], state_lru_h[l], state_conv_b[l], k_past, v_past)
        ks_l.append(k_n); vs_l.append(v_n); hs_l.append(h_n); cas_l.append(ca_n); cbs_l.append(cb_n)
    y_prompt = rmsnorm(xp, norm_final)
    y_sample = rmsnorm(xs, norm_final)
    return (y_prompt, y_sample,
            jnp.stack(kp_l), jnp.stack(vp_l), jnp.stack(ks_l), jnp.stack(vs_l),
            jnp.stack(hp_l), jnp.stack(hs_l),
            jnp.stack(cap_l), jnp.stack(cas_l),
            jnp.stack(cbp_l), jnp.stack(cbs_l))
```

```python
import functools

import jax
import jax.numpy as jnp
from jax import lax
from jax.experimental import pallas as pl
from jax.experimental.pallas import tpu as pltpu

F32 = jnp.float32
BF16 = jnp.bfloat16

RMS_EPS = 1e-6
LRU_C = 8.0
CONV_A_WIDTH = 4
CONV_B_WIDTH = 3
N_IN_PIECES = 8
N_BRANCH = 3

V7X_VMEM_LIMIT_BYTES = 56 * 1024 * 1024
SUBLANES = 8
LANES = 128
KV_TILE = 128
NEG_BIG = -1e30


def _params(*semantics):
    return pltpu.CompilerParams(dimension_semantics=semantics, vmem_limit_bytes=V7X_VMEM_LIMIT_BYTES)


def _resident(shape):
    zeros = (0,) * len(shape)
    return pl.BlockSpec(shape, lambda *_: zeros, pipeline_mode=pl.Buffered(1))


def _rms(x, g):
    return x * lax.rsqrt(jnp.mean(x * x, axis=-1, keepdims=True) + RMS_EPS) * g


def _dot(a, b):
    return jnp.dot(a, b, preferred_element_type=F32)


def _dot_nt(a, b):
    return lax.dot_general(a, b, (((1,), (1,)), ((), ())), preferred_element_type=F32)


def _softplus(z):
    return jnp.maximum(z, 0.0) + jnp.log(1.0 + jnp.exp(-jnp.abs(z)))


def _cumsum_matrix():
    j = lax.broadcasted_iota(jnp.int32, (2 * KV_TILE, 2 * KV_TILE), 0) % KV_TILE
    s = lax.broadcasted_iota(jnp.int32, (2 * KV_TILE, 2 * KV_TILE), 1)
    return jnp.where((s >= KV_TILE) | (j > s), 1.0, 0.0).astype(BF16)


def _stick_tile(z, valid, carry, cum_mat):
    sp = _softplus(z)
    if valid is not None:
        sp = jnp.where(valid, sp, 0.0)
    hi = sp.astype(BF16)
    lo = (sp - hi.astype(F32)).astype(BF16)
    sums = _dot(jnp.concatenate([hi, lo], axis=1), cum_mat)
    logw = z - sp - sums[:, :KV_TILE] - carry
    if valid is not None:
        logw = jnp.where(valid, logw, NEG_BIG)
    return jnp.exp(logw), carry + sums[:, KV_TILE:]


def _ffn_kernel(x_ref, g_ref, w1_ref, w3_ref, w2_ref, o_ref, *, f_chunk):
    x = x_ref[...]
    xn = _rms(x, g_ref[...]).astype(BF16)
    acc = jnp.zeros(x.shape, F32)
    for f0 in range(0, w1_ref.shape[1], f_chunk):
        h1 = _dot(xn, w1_ref[:, f0:f0 + f_chunk])
        h3 = _dot(xn, w3_ref[:, f0:f0 + f_chunk])
        act = (h1 * jax.nn.sigmoid(h1) * h3).astype(BF16)
        acc = acc + _dot(act, w2_ref[f0:f0 + f_chunk, :])
    o_ref[...] = x + 0.5 * acc


def _ffn(x, g, w1, w3, w2, *, tm):
    n, d = x.shape
    f = w1.shape[1]
    f_chunk = f // 2 if (f // 2) % LANES == 0 else f
    row = pl.BlockSpec((tm, d), lambda i: (i, 0))
    return pl.pallas_call(
        functools.partial(_ffn_kernel, f_chunk=f_chunk),
        grid=(n // tm,),
        in_specs=[row, _resident((1, d)), _resident((d, f)), _resident((d, f)), _resident((f, d))],
        out_specs=row,
        out_shape=jax.ShapeDtypeStruct((n, d), F32),
        compiler_params=_params("parallel"),
        name="ffn",
    )(x, g, w1, w3, w2)


def _norm_kernel(x_ref, g_ref, o_ref):
    o_ref[...] = _rms(x_ref[...], g_ref[...])


def _final_norm(x, g, *, tm):
    n, d = x.shape
    row = pl.BlockSpec((tm, d), lambda i: (i, 0))
    return pl.pallas_call(
        _norm_kernel, grid=(n // tm,), in_specs=[row, _resident((1, d))], out_specs=row,
        out_shape=jax.ShapeDtypeStruct((n, d), F32), compiler_params=_params("parallel"), name="final_norm",
    )(x, g)


def _lru_gates(xc, wa_ref, ba, wx_ref, bx, lam):
    heads, blk, _ = wa_ref.shape
    xcb = xc.astype(BF16)
    r_pre = jnp.concatenate([_dot(xcb[:, h * blk:(h + 1) * blk], wa_ref[h]) for h in range(heads)], axis=1)
    i_pre = jnp.concatenate([_dot(xcb[:, h * blk:(h + 1) * blk], wx_ref[h]) for h in range(heads)], axis=1)
    r = jax.nn.sigmoid(r_pre + ba)
    i = jax.nn.sigmoid(i_pre + bx)
    log_a = (-LRU_C * _softplus(-lam)) * r
    a = jnp.exp(log_a)
    mult = jnp.sqrt(-jnp.tanh(log_a) * (a * a + 1.0))
    return a, mult * (i * xc)


def _mixer_in_prompt_kernel(x_ref, g_ref, w_in_ref, caw_ref, cab_ref, wa_ref, ba_ref, wx_ref, bx_ref, lam_ref,
                            cbw_ref,
                            ya_ref, yb_ref, q_ref, k_ref, v_ref, kb_ref, vb_ref, h_ref, ca_ref, cb_ref,
                            ax_ext, u_ext, hseq, h_carry, *, q_scale):
    tm = x_ref.shape[0]
    w = ya_ref.shape[1]
    pad = SUBLANES

    @pl.when(pl.program_id(1) == 0)
    def _():
        ax_ext[0:pad, :] = jnp.zeros((pad, w), F32)
        u_ext[0:pad, :] = jnp.zeros((pad, w), F32)
        h_carry[...] = jnp.zeros(h_carry.shape, F32)

    xn = _rms(x_ref[...], g_ref[...]).astype(BF16)

    def piece(p):
        return _dot(xn, w_in_ref[:, p * w:(p + 1) * w])

    ax_ext[pad:pad + tm, :] = piece(0)
    xc = cab_ref[...]
    for j in range(CONV_A_WIDTH):
        off = pad - (CONV_A_WIDTH - 1) + j
        xc = xc + ax_ext[off:off + tm, :] * caw_ref[j:j + 1, :]
    ca_ref[...] = ax_ext[pad + tm - (CONV_A_WIDTH - 1):pad + tm, :]
    ax_ext[0:pad, :] = ax_ext[tm:tm + pad, :]

    a, b = _lru_gates(xc, wa_ref, ba_ref[...], wx_ref, bx_ref[...], lam_ref[...])
    row = lax.broadcasted_iota(jnp.int32, (tm, w), 0) % SUBLANES
    for k in (1, 2, 4):
        keep = row >= k
        b = jnp.where(keep, b + a * pltpu.roll(b, k, 0), b)
        a = jnp.where(keep, a * pltpu.roll(a, k, 0), a)
    h = h_carry[0:1, :]
    for r0 in range(0, tm, SUBLANES):
        blk = b[r0:r0 + SUBLANES] + a[r0:r0 + SUBLANES] * h
        hseq[r0:r0 + SUBLANES, :] = blk
        h = blk[SUBLANES - 1:SUBLANES]
    h_carry[0:1, :] = h
    h_ref[...] = h
    ya_ref[...] = (hseq[...] * jax.nn.gelu(piece(1))).astype(BF16)

    u_ext[pad:pad + tm, :] = piece(3) * piece(4)
    cu = jnp.zeros((tm, w), F32)
    for j in range(CONV_B_WIDTH):
        off = pad - (CONV_B_WIDTH - 1) + j
        cu = cu + u_ext[off:off + tm, :] * cbw_ref[j:j + 1, :]
    cb_ref[...] = u_ext[pad + tm - (CONV_B_WIDTH - 1):pad + tm, :]
    u_ext[0:pad, :] = u_ext[tm:tm + pad, :]
    yb_ref[...] = (piece(2) * cu).astype(BF16)

    q_ref[...] = (piece(5) * q_scale).astype(BF16)
    k = piece(6)
    k_ref[...] = k
    kb_ref[...] = k.astype(BF16)
    v = piece(7)
    v_ref[...] = v
    vb_ref[...] = v.astype(BF16)


def _mixer_in_prompt(x, lw, *, tm, q_scale):
    bsz, t, d = x.shape
    w = lw["w_in"].shape[1] // N_IN_PIECES
    heads, blk, _ = lw["rg_w_a"].shape
    tile = lambda: pl.BlockSpec((None, tm, w), lambda b, i: (b, i, 0))
    state = lambda rows: pl.BlockSpec((None, rows, w), lambda b, i: (b, 0, 0))
    seq = lambda dt: jax.ShapeDtypeStruct((bsz, t, w), dt)
    return pl.pallas_call(
        functools.partial(_mixer_in_prompt_kernel, q_scale=q_scale),
        grid=(bsz, t // tm),
        in_specs=[pl.BlockSpec((None, tm, d), lambda b, i: (b, i, 0)), _resident((1, d)),
                  _resident((d, N_IN_PIECES * w)), _resident((CONV_A_WIDTH, w)), _resident((1, w)),
                  _resident((heads, blk, blk)), _resident((1, w)), _resident((heads, blk, blk)),
                  _resident((1, w)), _resident((1, w)), _resident((CONV_B_WIDTH, w))],
        out_specs=[tile(), tile(), tile(), tile(), tile(), tile(), tile(),
                   state(1), state(CONV_A_WIDTH - 1), state(CONV_B_WIDTH - 1)],
        out_shape=[seq(BF16), seq(BF16), seq(BF16), seq(F32), seq(F32), seq(BF16), seq(BF16),
                   jax.ShapeDtypeStruct((bsz, 1, w), F32),
                   jax.ShapeDtypeStruct((bsz, CONV_A_WIDTH - 1, w), F32),
                   jax.ShapeDtypeStruct((bsz, CONV_B_WIDTH - 1, w), F32)],
        scratch_shapes=[pltpu.VMEM((tm + SUBLANES, w), F32), pltpu.VMEM((tm + SUBLANES, w), F32),
                        pltpu.VMEM((tm, w), F32), pltpu.VMEM((SUBLANES, w), F32)],
        compiler_params=_params("parallel", "arbitrary"),
        name="mixer_in_prompt",
    )(x, lw["norm_mix"], lw["w_in"], lw["conv_a_w"], lw["conv_a_b"], lw["rg_w_a"], lw["rg_b_a"],
      lw["rg_w_x"], lw["rg_b_x"], lw["rg_lambda"], lw["conv_b_w"])


def _mixer_in_sample_kernel(x_ref, g_ref, w_in_ref, caw_ref, cab_ref, wa_ref, ba_ref, wx_ref, bx_ref, lam_ref,
                            cbw_ref, sa_ref, h0_ref, sb_ref,
                            ya_ref, yb_ref, q_ref, k_ref, v_ref, h_ref, ca_ref, cb_ref, *, q_scale, steps):
    n = x_ref.shape[0]
    bn = n // steps
    w = ya_ref.shape[1]
    xn = _rms(x_ref[...], g_ref[...]).astype(BF16)

    def piece(p):
        return _dot(xn, w_in_ref[:, p * w:(p + 1) * w])

    def slabs(u):
        return [u[s * bn:(s + 1) * bn] for s in range(steps)]

    def conv(history, weights_ref, width):
        out = []
        for s in range(steps):
            y = history[s] * weights_ref[0:1, :]
            for j in range(1, width):
                y = y + history[s + j] * weights_ref[j:j + 1, :]
            out.append(y)
        return jnp.concatenate(out, axis=0)

    hist_a = [sa_ref[j] for j in range(CONV_A_WIDTH - 1)] + slabs(piece(0))
    xc = conv(hist_a, caw_ref, CONV_A_WIDTH) + cab_ref[...]
    for j in range(CONV_A_WIDTH - 1):
        ca_ref[j] = hist_a[steps + j]
    a, b = _lru_gates(xc, wa_ref, ba_ref[...], wx_ref, bx_ref[...], lam_ref[...])
    h = h0_ref[...]
    hs = []
    for s in range(steps):
        h = a[s * bn:(s + 1) * bn] * h + b[s * bn:(s + 1) * bn]
        hs.append(h)
    h_ref[...] = h
    ya_ref[...] = (jnp.concatenate(hs, axis=0) * jax.nn.gelu(piece(1))).astype(BF16)

    hist_b = [sb_ref[j] for j in range(CONV_B_WIDTH - 1)] + slabs(piece(3) * piece(4))
    cu = conv(hist_b, cbw_ref, CONV_B_WIDTH)
    for j in range(CONV_B_WIDTH - 1):
        cb_ref[j] = hist_b[steps + j]
    yb_ref[...] = (piece(2) * cu).astype(BF16)

    q_ref[...] = piece(5) * q_scale
    k_ref[...] = piece(6)
    v_ref[...] = piece(7)


def _mixer_in_sample(x, lw, conv_a_state, h0, conv_b_state, *, q_scale, steps):
    n, d = x.shape
    bn = n // steps
    w = lw["w_in"].shape[1] // N_IN_PIECES
    heads, blk, _ = lw["rg_w_a"].shape
    full = lambda shape: pl.BlockSpec(shape, lambda i: (0,) * len(shape))
    rows = lambda dt: jax.ShapeDtypeStruct((n, w), dt)
    return pl.pallas_call(
        functools.partial(_mixer_in_sample_kernel, q_scale=q_scale, steps=steps),
        grid=(1,),
        in_specs=[full((n, d)), _resident((1, d)), _resident((d, N_IN_PIECES * w)),
                  _resident((CONV_A_WIDTH, w)), _resident((1, w)),
                  _resident((heads, blk, blk)), _resident((1, w)), _resident((heads, blk, blk)),
                  _resident((1, w)), _resident((1, w)), _resident((CONV_B_WIDTH, w)),
                  full((CONV_A_WIDTH - 1, bn, w)), full((bn, w)), full((CONV_B_WIDTH - 1, bn, w))],
        out_specs=[full((n, w)), full((n, w)), full((n, w)), full((n, w)), full((n, w)),
                   full((bn, w)), full((CONV_A_WIDTH - 1, bn, w)), full((CONV_B_WIDTH - 1, bn, w))],
        out_shape=[rows(BF16), rows(BF16), rows(F32), rows(F32), rows(F32),
                   jax.ShapeDtypeStruct((bn, w), F32),
                   jax.ShapeDtypeStruct((CONV_A_WIDTH - 1, bn, w), F32),
                   jax.ShapeDtypeStruct((CONV_B_WIDTH - 1, bn, w), F32)],
        compiler_params=_params("arbitrary"),
        name="mixer_in_sample",
    )(x, lw["norm_mix"], lw["w_in"], lw["conv_a_w"], lw["conv_a_b"], lw["rg_w_a"], lw["rg_b_a"],
      lw["rg_w_x"], lw["rg_b_x"], lw["rg_lambda"], lw["conv_b_w"], conv_a_state, h0, conv_b_state)


def _attn_prompt_kernel(bias_ref, q_ref, k_ref, v_ref, cum_ref, o_ref, carry_ref, acc_ref, *, heads):
    tq = q_ref.shape[0]
    dh = q_ref.shape[1] // heads
    qi = pl.program_id(1)
    carry_ref[...] = jnp.zeros(carry_ref.shape, F32)
    acc_ref[...] = jnp.zeros(acc_ref.shape, F32)
    cum_mat = cum_ref[...]

    def tile(j, valid):
        k0 = pl.multiple_of(j * KV_TILE, KV_TILE)
        for h in range(heads):
            cols = slice(h * dh, (h + 1) * dh)
            z = _dot_nt(q_ref[:, cols], k_ref[pl.ds(k0, KV_TILE), cols]) + bias_ref[h]
            wts, carry = _stick_tile(z, valid, carry_ref[h], cum_mat)
            carry_ref[h] = carry
            acc_ref[h] += _dot(wts.astype(BF16), v_ref[pl.ds(k0, KV_TILE), cols])

    diag = (lax.broadcasted_iota(jnp.int32, (tq, KV_TILE), 1) < lax.broadcasted_iota(jnp.int32, (tq, KV_TILE), 0))
    tile(qi, diag)

    def body(step, _):
        tile(qi - 1 - step, None)
        return 0

    lax.fori_loop(0, qi, body, 0)
    for h in range(heads):
        o_ref[:, h * dh:(h + 1) * dh] = acc_ref[h].astype(o_ref.dtype)


def _attn_prompt(q, k, v, bias, cum_mat, *, heads):
    bsz, t, w = q.shape
    tq = KV_TILE
    return pl.pallas_call(
        functools.partial(_attn_prompt_kernel, heads=heads),
        grid=(bsz, t // tq),
        in_specs=[pl.BlockSpec(memory_space=pltpu.SMEM),
                  pl.BlockSpec((None, tq, w), lambda b, i: (b, i, 0)),
                  pl.BlockSpec((None, t, w), lambda b, i: (b, 0, 0)),
                  pl.BlockSpec((None, t, w), lambda b, i: (b, 0, 0)),
                  _resident(cum_mat.shape)],
        out_specs=pl.BlockSpec((None, tq, w), lambda b, i: (b, i, 0)),
        out_shape=jax.ShapeDtypeStruct((bsz, t, w), BF16),
        scratch_shapes=[pltpu.VMEM((heads, tq, KV_TILE), F32), pltpu.VMEM((heads, tq, w // heads), F32)],
        compiler_params=_params("parallel", "arbitrary"),
        name="attn_prompt",
    )(bias, q, k, v, cum_mat)


def _attn_sample_kernel(layer_ref, pt_ref, q_ref, kn_ref, vn_ref, kp_ref, vp_ref, bias_ref, cum_ref, o_ref,
                        qrows, knew, vnew, carry_ref, acc_ref, *, heads, steps):
    del layer_ref, pt_ref
    w = q_ref.shape[1]
    dh = w // heads
    rows = steps * heads
    j = pl.program_id(1)
    cum_mat = cum_ref[...]
    bias = bias_ref[...]

    def tile(k_tile, v_tile, valid):
        z = _dot_nt(qrows[...], k_tile) + bias
        wts, carry = _stick_tile(z, valid, carry_ref[...], cum_mat)
        carry_ref[...] = carry
        acc_ref[...] += _dot(wts.astype(BF16), v_tile)

    @pl.when((j == 0) & (pl.program_id(0) == 0))
    def _():
        knew[...] = jnp.zeros(knew.shape, F32)
        vnew[...] = jnp.zeros(vnew.shape, F32)

    @pl.when(j == 0)
    def _():
        head_of_lane = lax.broadcasted_iota(jnp.int32, (heads, w), 1) // dh
        head_of_row = lax.broadcasted_iota(jnp.int32, (heads, w), 0)
        for s in range(steps):
            q_s = jnp.broadcast_to(q_ref[s:s + 1, :], (heads, w))
            qrows[s * heads:(s + 1) * heads, :] = jnp.where(head_of_lane == head_of_row, q_s, 0.0).astype(BF16)
        carry_ref[...] = jnp.zeros(carry_ref.shape, F32)
        acc_ref[...] = jnp.zeros(acc_ref.shape, F32)
        knew[0:SUBLANES, :] = kn_ref[...]
        vnew[0:SUBLANES, :] = vn_ref[...]
        key = lax.broadcasted_iota(jnp.int32, (rows, KV_TILE), 1)
        step_of_row = lax.broadcasted_iota(jnp.int32, (rows, KV_TILE), 0) // heads
        tile(knew[...].astype(BF16), vnew[...].astype(BF16), key < step_of_row)

    tile(kp_ref[...].astype(BF16), vp_ref[...].astype(BF16), None)

    @pl.when(j == pl.num_programs(1) - 1)
    def _():
        head_of_lane = lax.broadcasted_iota(jnp.int32, (heads, w), 1) // dh
        head_of_row = lax.broadcasted_iota(jnp.int32, (heads, w), 0)
        own = head_of_lane == head_of_row
        for s in range(steps):
            blk = acc_ref[s * heads:(s + 1) * heads, :]
            o_ref[s:s + 1, :] = jnp.sum(jnp.where(own, blk, 0.0), axis=0, keepdims=True)


def _attn_sample(layer, page_table, q, k_new, v_new, cache_k, cache_v, bias_rows, cum_mat, *, heads, steps):
    bn, _, w = q.shape
    n_pages = page_table.shape[0] // bn
    page = cache_k.shape[2]
    assert page == KV_TILE
    rows = steps * heads
    page_spec = pl.BlockSpec((None, None, page, w),
                             lambda b, j, layer_ref, pt_ref: (layer_ref[0], pt_ref[b * n_pages + n_pages - 1 - j], 0, 0))
    per_seq = lambda r: pl.BlockSpec((None, r, w), lambda b, j, *_: (b, 0, 0))
    const = lambda shape: pl.BlockSpec(shape, lambda b, j, *_: (0,) * len(shape))
    grid_spec = pltpu.PrefetchScalarGridSpec(
        num_scalar_prefetch=2,
        grid=(bn, n_pages),
        in_specs=[per_seq(steps), per_seq(SUBLANES), per_seq(SUBLANES), page_spec, page_spec,
                  const((rows, 1)), const(cum_mat.shape)],
        out_specs=per_seq(steps),
        scratch_shapes=[pltpu.VMEM((rows, w), BF16), pltpu.VMEM((KV_TILE, w), F32), pltpu.VMEM((KV_TILE, w), F32),
                        pltpu.VMEM((rows, KV_TILE), F32), pltpu.VMEM((rows, w), F32)],
    )
    return pl.pallas_call(
        functools.partial(_attn_sample_kernel, heads=heads, steps=steps),
        grid_spec=grid_spec,
        out_shape=jax.ShapeDtypeStruct((bn, steps, w), F32),
        compiler_params=_params("arbitrary", "arbitrary"),
        name="attn_sample",
    )(layer, page_table, q, k_new, v_new, cache_k, cache_v, bias_rows, cum_mat)


def _mixer_out_kernel(x_ref, g_ref, ya_ref, yb_ref, yc_ref, wbr_ref, wg_ref, bg_ref, wo_ref, o_ref):
    x = x_ref[...]
    d = x.shape[1]
    xn = _rms(x, g_ref[...]).astype(BF16)
    mixed = jnp.zeros(x.shape, F32)
    for g, y_ref in enumerate((ya_ref, yb_ref, yc_ref)):
        gate = jax.nn.sigmoid(_dot(xn, wg_ref[:, g * d:(g + 1) * d]) + bg_ref[:, g * d:(g + 1) * d])
        mixed = mixed + gate * _dot(y_ref[...], wbr_ref[g])
    o_ref[...] = x + _dot(mixed.astype(BF16), wo_ref[...])


def _mixer_out(x, ya, yb, yc, lw, *, tm):
    n, d = x.shape
    w = ya.shape[1]
    row = lambda c: pl.BlockSpec((tm, c), lambda i: (i, 0))
    return pl.pallas_call(
        _mixer_out_kernel,
        grid=(n // tm,),
        in_specs=[row(d), _resident((1, d)), row(w), row(w), row(w), _resident((N_BRANCH, w, d)),
                  _resident((d, N_BRANCH * d)), _resident((1, N_BRANCH * d)), _resident((d, d))],
        out_specs=row(d),
        out_shape=jax.ShapeDtypeStruct((n, d), F32),
        compiler_params=_params("parallel"),
        name="mixer_out",
    )(x, lw["norm_mix"], ya, yb, yc, lw["w_branch"], lw["w_gate"], lw["b_gate"], lw["w_o"])


def _row_tile(n, want):
    tm = min(n, want)
    assert n % tm == 0
    return tm


def kernel(x_prompt, x_sample, cache_k, cache_v, page_table, state_lru_h, state_conv_a, state_conv_b, norm_ffn1, ffn1_w1, ffn1_w3, ffn1_w2, norm_mix, w_in, conv_a_w, conv_a_b, rg_w_a, rg_b_a, rg_w_x, rg_b_x, rg_lambda, conv_b_w, sb_bias, w_branch, w_gate, b_gate, w_o, norm_ffn2, ffn2_w1, ffn2_w3, ffn2_w2, norm_final):
    depth = w_in.shape[0]
    bp, tp, d = x_prompt.shape
    bn, steps, _ = x_sample.shape
    heads, dh = cache_k.shape[3], cache_k.shape[4]
    w = heads * dh
    n_phys, page = cache_k.shape[1], cache_k.shape[2]
    q_scale = float(dh) ** -0.5
    cum_mat = _cumsum_matrix()

    row2 = lambda a: a.reshape(depth, 1, -1)
    weights = dict(
        norm_ffn1=row2(norm_ffn1), ffn1_w1=ffn1_w1.astype(BF16), ffn1_w3=ffn1_w3.astype(BF16),
        ffn1_w2=ffn1_w2.astype(BF16), norm_mix=row2(norm_mix), w_in=w_in.astype(BF16), conv_a_w=conv_a_w,
        conv_a_b=row2(conv_a_b), rg_w_a=rg_w_a.astype(BF16), rg_b_a=row2(rg_b_a), rg_w_x=rg_w_x.astype(BF16),
        rg_b_x=row2(rg_b_x), rg_lambda=row2(rg_lambda), conv_b_w=conv_b_w, w_branch=w_branch.astype(BF16),
        w_gate=w_gate.astype(BF16), b_gate=row2(b_gate), w_o=w_o.astype(BF16), norm_ffn2=row2(norm_ffn2),
        ffn2_w1=ffn2_w1.astype(BF16), ffn2_w3=ffn2_w3.astype(BF16), ffn2_w2=ffn2_w2.astype(BF16))

    cache_k = cache_k.reshape(depth, n_phys, page, w)
    cache_v = cache_v.reshape(depth, n_phys, page, w)
    page_flat = page_table.reshape(-1)

    n_p = bp * tp
    n_s = bn * steps
    tm_p = _row_tile(n_p, 512)
    tm_s = _row_tile(n_s, 512)
    tm_mix = _row_tile(tp, 256)

    xp = x_prompt.reshape(n_p, d)
    xs = x_sample.transpose(1, 0, 2).reshape(n_s, d)
    outs = [[] for _ in range(10)]

    def to_seq_major(a):
        return a.reshape(steps, bn, -1).transpose(1, 0, 2)

    for l in range(depth):
        lw = {name: val[l] for name, val in weights.items()}

        xp = _ffn(xp, lw["norm_ffn1"], lw["ffn1_w1"], lw["ffn1_w3"], lw["ffn1_w2"], tm=tm_p)
        ya, yb, q, k, v, kb, vb, h_last, ca_new, cb_new = _mixer_in_prompt(
            xp.reshape(bp, tp, d), lw, tm=tm_mix, q_scale=q_scale)
        yc = _attn_prompt(q, kb, vb, sb_bias[l], cum_mat, heads=heads)
        xp = _mixer_out(xp, ya.reshape(n_p, w), yb.reshape(n_p, w), yc.reshape(n_p, w), lw, tm=tm_p)
        xp = _ffn(xp, lw["norm_ffn2"], lw["ffn2_w1"], lw["ffn2_w3"], lw["ffn2_w2"], tm=tm_p)
        outs[0].append(k.reshape(bp, tp, heads, dh))
        outs[1].append(v.reshape(bp, tp, heads, dh))
        outs[4].append(h_last.reshape(bp, w))
        outs[6].append(ca_new)
        outs[8].append(cb_new)

        xs = _ffn(xs, lw["norm_ffn1"], lw["ffn1_w1"], lw["ffn1_w3"], lw["ffn1_w2"], tm=tm_s)
        ya, yb, q, k, v, h_last, ca_new, cb_new = _mixer_in_sample(
            xs, lw, state_conv_a[l].transpose(1, 0, 2), state_lru_h[l], state_conv_b[l].transpose(1, 0, 2),
            q_scale=q_scale, steps=steps)
        k_sm, v_sm = to_seq_major(k), to_seq_major(v)
        pad_rows = ((0, 0), (0, SUBLANES - steps), (0, 0))
        bias_rows = jnp.tile(sb_bias[l], steps).reshape(steps * heads, 1)
        yc = _attn_sample(jnp.full((1,), l, jnp.int32), page_flat, to_seq_major(q),
                          jnp.pad(k_sm, pad_rows), jnp.pad(v_sm, pad_rows), cache_k, cache_v,
                          bias_rows, cum_mat, heads=heads, steps=steps)
        yc = yc.transpose(1, 0, 2).reshape(n_s, w).astype(BF16)
        xs = _mixer_out(xs, ya, yb, yc, lw, tm=tm_s)
        xs = _ffn(xs, lw["norm_ffn2"], lw["ffn2_w1"], lw["ffn2_w3"], lw["ffn2_w2"], tm=tm_s)
        outs[2].append(k_sm.reshape(bn, steps, heads, dh))
        outs[3].append(v_sm.reshape(bn, steps, heads, dh))
        outs[5].append(h_last)
        outs[7].append(ca_new.transpose(1, 0, 2))
        outs[9].append(cb_new.transpose(1, 0, 2))

    g_final = norm_final.reshape(1, d)
    y_prompt = _final_norm(xp, g_final, tm=tm_p).reshape(bp, tp, d)
    y_sample = _final_norm(xs, g_final, tm=tm_s).reshape(steps, bn, d).transpose(1, 0, 2)
    return (y_prompt, y_sample) + tuple(jnp.stack(o) for o in outs)
```

```python
import functools

import jax
import jax.numpy as jnp
from jax import lax
from jax.experimental import pallas as pl
from jax.experimental.pallas import tpu as pltpu

F32 = jnp.float32
BF16 = jnp.bfloat16

RMS_EPS = 1e-6
LRU_C = 8.0
CONV_A_WIDTH = 4
CONV_B_WIDTH = 3
N_IN_PIECES = 8
N_BRANCH = 3

V7X_VMEM_LIMIT_BYTES = 56 * 1024 * 1024
SUBLANES = 8
LANES = 128
KV_TILE = 128
NEG_BIG = -1e30
LOG2_E = 1.4426950408889634


def _params(*semantics):
    return pltpu.CompilerParams(dimension_semantics=semantics, vmem_limit_bytes=V7X_VMEM_LIMIT_BYTES)


def _resident(shape):
    zeros = (0,) * len(shape)
    return pl.BlockSpec(shape, lambda *_: zeros, pipeline_mode=pl.Buffered(1))


def _rms(x, g):
    return x * lax.rsqrt(jnp.mean(x * x, axis=-1, keepdims=True) + RMS_EPS) * g


def _dot(a, b):
    return jnp.dot(a, b, preferred_element_type=F32)


def _dot_nt(a, b):
    return lax.dot_general(a, b, (((1,), (1,)), ((), ())), preferred_element_type=F32)


def _softplus(z):
    return jnp.maximum(z, 0.0) + jnp.log(1.0 + jnp.exp(-jnp.abs(z)))


def _cumsum_matrix():
    j = lax.broadcasted_iota(jnp.int32, (2 * KV_TILE, 2 * KV_TILE), 0) % KV_TILE
    s = lax.broadcasted_iota(jnp.int32, (2 * KV_TILE, 2 * KV_TILE), 1)
    return jnp.where((s >= KV_TILE) | (j > s), 1.0, 0.0).astype(BF16)


def _stick_prepare(z, valid):
    neg_abs = lax.bitcast_convert_type(lax.bitcast_convert_type(z, jnp.uint32) | jnp.uint32(0x80000000), F32)
    sp = jnp.maximum(z, 0.0) + jnp.log(1.0 + jnp.exp2(neg_abs)) * LOG2_E
    t = z - sp
    if valid is not None:
        sp = jnp.where(valid, sp, 0.0)
        t = jnp.where(valid, t, NEG_BIG)
    hi = lax.bitcast_convert_type(lax.bitcast_convert_type(sp, jnp.uint32) & jnp.uint32(0xFFFF0000), F32)
    lo = (sp - hi).astype(BF16)
    hi = hi.astype(BF16)
    tiles = range(0, z.shape[1], KV_TILE)
    return t, [jnp.concatenate([hi[:, c:c + KV_TILE], lo[:, c:c + KV_TILE]], axis=1) for c in tiles]


def _stick_sums(carry, sp_operand, cum_mat):
    r = jnp.concatenate([carry, carry], axis=1) + _dot(sp_operand, cum_mat)
    return r[:, :KV_TILE], r[:, KV_TILE:]


def _ffn_kernel(x_ref, g_ref, w1_ref, w3_ref, w2_ref, o_ref, *, f_chunk):
    x = x_ref[...]
    xn = _rms(x, g_ref[...]).astype(BF16)
    acc = jnp.zeros(x.shape, F32)
    for f0 in range(0, w1_ref.shape[1], f_chunk):
        h1 = _dot(xn, w1_ref[:, f0:f0 + f_chunk])
        h3 = _dot(xn, w3_ref[:, f0:f0 + f_chunk])
        act = (h1 * jax.nn.sigmoid(h1) * h3).astype(BF16)
        acc = acc + _dot(act, w2_ref[f0:f0 + f_chunk, :])
    o_ref[...] = x + 0.5 * acc


def _ffn(x, g, w1, w3, w2, *, tm):
    n, d = x.shape
    f = w1.shape[1]
    f_chunk = f // 2 if (f // 2) % LANES == 0 else f
    row = pl.BlockSpec((tm, d), lambda i: (i, 0))
    return pl.pallas_call(
        functools.partial(_ffn_kernel, f_chunk=f_chunk),
        grid=(n // tm,),
        in_specs=[row, _resident((1, d)), _resident((d, f)), _resident((d, f)), _resident((f, d))],
        out_specs=row,
        out_shape=jax.ShapeDtypeStruct((n, d), F32),
        compiler_params=_params("parallel"),
        name="ffn",
    )(x, g, w1, w3, w2)


def _norm_kernel(x_ref, g_ref, o_ref):
    o_ref[...] = _rms(x_ref[...], g_ref[...])


def _final_norm(x, g, *, tm):
    n, d = x.shape
    row = pl.BlockSpec((tm, d), lambda i: (i, 0))
    return pl.pallas_call(
        _norm_kernel, grid=(n // tm,), in_specs=[row, _resident((1, d))], out_specs=row,
        out_shape=jax.ShapeDtypeStruct((n, d), F32), compiler_params=_params("parallel"), name="final_norm",
    )(x, g)


def _lru_gates(xc, wa_ref, ba, wx_ref, bx, lam):
    heads, blk, _ = wa_ref.shape
    xcb = xc.astype(BF16)
    r_pre = jnp.concatenate([_dot(xcb[:, h * blk:(h + 1) * blk], wa_ref[h]) for h in range(heads)], axis=1)
    i_pre = jnp.concatenate([_dot(xcb[:, h * blk:(h + 1) * blk], wx_ref[h]) for h in range(heads)], axis=1)
    r = jax.nn.sigmoid(r_pre + ba)
    i = jax.nn.sigmoid(i_pre + bx)
    log_a = (-LRU_C * _softplus(-lam)) * r
    a = jnp.exp(log_a)
    mult = jnp.sqrt(-jnp.tanh(log_a) * (a * a + 1.0))
    return a, mult * (i * xc)


def _mixer_in_prompt_kernel(x_ref, g_ref, w_in_ref, caw_ref, cab_ref, wa_ref, ba_ref, wx_ref, bx_ref, lam_ref,
                            cbw_ref,
                            ya_ref, yb_ref, q_ref, k_ref, v_ref, kb_ref, vb_ref, h_ref, ca_ref, cb_ref,
                            ax_ext, u_ext, hseq, h_carry, *, q_scale):
    tm = x_ref.shape[0]
    w = ya_ref.shape[1]
    pad = SUBLANES

    @pl.when(pl.program_id(1) == 0)
    def _():
        ax_ext[0:pad, :] = jnp.zeros((pad, w), F32)
        u_ext[0:pad, :] = jnp.zeros((pad, w), F32)
        h_carry[...] = jnp.zeros(h_carry.shape, F32)

    xn = _rms(x_ref[...], g_ref[...]).astype(BF16)

    def piece(p):
        return _dot(xn, w_in_ref[:, p * w:(p + 1) * w])

    ax_ext[pad:pad + tm, :] = piece(0)
    xc = cab_ref[...]
    for j in range(CONV_A_WIDTH):
        off = pad - (CONV_A_WIDTH - 1) + j
        xc = xc + ax_ext[off:off + tm, :] * caw_ref[j:j + 1, :]
    ca_ref[...] = ax_ext[pad + tm - (CONV_A_WIDTH - 1):pad + tm, :]
    ax_ext[0:pad, :] = ax_ext[tm:tm + pad, :]

    a, b = _lru_gates(xc, wa_ref, ba_ref[...], wx_ref, bx_ref[...], lam_ref[...])

    q_ref[...] = (piece(5) * q_scale).astype(BF16)
    k = piece(6)
    k_ref[...] = k
    kb_ref[...] = k.astype(BF16)
    v = piece(7)
    v_ref[...] = v
    vb_ref[...] = v.astype(BF16)
    u_ext[pad:pad + tm, :] = piece(3) * piece(4)
    sc_b = piece(2)
    a_gate = piece(1)

    row = lax.broadcasted_iota(jnp.int32, (tm, w), 0) % SUBLANES
    for k in (1, 2, 4):
        keep = row >= k
        b = jnp.where(keep, b + a * pltpu.roll(b, k, 0), b)
        a = jnp.where(keep, a * pltpu.roll(a, k, 0), a)
    h = h_carry[0:1, :]
    for r0 in range(0, tm, SUBLANES):
        blk = b[r0:r0 + SUBLANES] + a[r0:r0 + SUBLANES] * h
        hseq[r0:r0 + SUBLANES, :] = blk
        h = blk[SUBLANES - 1:SUBLANES]
    h_carry[0:1, :] = h
    h_ref[...] = h
    ya_ref[...] = (hseq[...] * jax.nn.gelu(a_gate)).astype(BF16)

    cu = jnp.zeros((tm, w), F32)
    for j in range(CONV_B_WIDTH):
        off = pad - (CONV_B_WIDTH - 1) + j
        cu = cu + u_ext[off:off + tm, :] * cbw_ref[j:j + 1, :]
    cb_ref[...] = u_ext[pad + tm - (CONV_B_WIDTH - 1):pad + tm, :]
    u_ext[0:pad, :] = u_ext[tm:tm + pad, :]
    yb_ref[...] = (sc_b * cu).astype(BF16)


def _mixer_in_prompt(x, lw, *, tm, q_scale):
    bsz, t, d = x.shape
    w = lw["w_in"].shape[1] // N_IN_PIECES
    heads, blk, _ = lw["rg_w_a"].shape
    tile = lambda: pl.BlockSpec((None, tm, w), lambda b, i: (b, i, 0))
    state = lambda rows: pl.BlockSpec((None, rows, w), lambda b, i: (b, 0, 0))
    seq = lambda dt: jax.ShapeDtypeStruct((bsz, t, w), dt)
    return pl.pallas_call(
        functools.partial(_mixer_in_prompt_kernel, q_scale=q_scale),
        grid=(bsz, t // tm),
        in_specs=[pl.BlockSpec((None, tm, d), lambda b, i: (b, i, 0)), _resident((1, d)),
                  _resident((d, N_IN_PIECES * w)), _resident((CONV_A_WIDTH, w)), _resident((1, w)),
                  _resident((heads, blk, blk)), _resident((1, w)), _resident((heads, blk, blk)),
                  _resident((1, w)), _resident((1, w)), _resident((CONV_B_WIDTH, w))],
        out_specs=[tile(), tile(), tile(), tile(), tile(), tile(), tile(),
                   state(1), state(CONV_A_WIDTH - 1), state(CONV_B_WIDTH - 1)],
        out_shape=[seq(BF16), seq(BF16), seq(BF16), seq(F32), seq(F32), seq(BF16), seq(BF16),
                   jax.ShapeDtypeStruct((bsz, 1, w), F32),
                   jax.ShapeDtypeStruct((bsz, CONV_A_WIDTH - 1, w), F32),
                   jax.ShapeDtypeStruct((bsz, CONV_B_WIDTH - 1, w), F32)],
        scratch_shapes=[pltpu.VMEM((tm + SUBLANES, w), F32), pltpu.VMEM((tm + SUBLANES, w), F32),
                        pltpu.VMEM((tm, w), F32), pltpu.VMEM((SUBLANES, w), F32)],
        compiler_params=_params("parallel", "arbitrary"),
        name="mixer_in_prompt",
    )(x, lw["norm_mix"], lw["w_in"], lw["conv_a_w"], lw["conv_a_b"], lw["rg_w_a"], lw["rg_b_a"],
      lw["rg_w_x"], lw["rg_b_x"], lw["rg_lambda"], lw["conv_b_w"])


def _mixer_in_sample_kernel(x_ref, g_ref, w_in_ref, caw_ref, cab_ref, wa_ref, ba_ref, wx_ref, bx_ref, lam_ref,
                            cbw_ref, sa_ref, h0_ref, sb_ref,
                            ya_ref, yb_ref, q_ref, k_ref, v_ref, h_ref, ca_ref, cb_ref, *, q_scale, steps):
    n = x_ref.shape[0]
    bn = n // steps
    w = ya_ref.shape[1]
    xn = _rms(x_ref[...], g_ref[...]).astype(BF16)

    def piece(p):
        return _dot(xn, w_in_ref[:, p * w:(p + 1) * w])

    def slabs(u):
        return [u[s * bn:(s + 1) * bn] for s in range(steps)]

    def conv(history, weights_ref, width):
        out = []
        for s in range(steps):
            y = history[s] * weights_ref[0:1, :]
            for j in range(1, width):
                y = y + history[s + j] * weights_ref[j:j + 1, :]
            out.append(y)
        return jnp.concatenate(out, axis=0)

    hist_a = [sa_ref[j] for j in range(CONV_A_WIDTH - 1)] + slabs(piece(0))
    xc = conv(hist_a, caw_ref, CONV_A_WIDTH) + cab_ref[...]
    for j in range(CONV_A_WIDTH - 1):
        ca_ref[j] = hist_a[steps + j]
    a, b = _lru_gates(xc, wa_ref, ba_ref[...], wx_ref, bx_ref[...], lam_ref[...])
    h = h0_ref[...]
    hs = []
    for s in range(steps):
        h = a[s * bn:(s + 1) * bn] * h + b[s * bn:(s + 1) * bn]
        hs.append(h)
    h_ref[...] = h
    ya_ref[...] = (jnp.concatenate(hs, axis=0) * jax.nn.gelu(piece(1))).astype(BF16)

    hist_b = [sb_ref[j] for j in range(CONV_B_WIDTH - 1)] + slabs(piece(3) * piece(4))
    cu = conv(hist_b, cbw_ref, CONV_B_WIDTH)
    for j in range(CONV_B_WIDTH - 1):
        cb_ref[j] = hist_b[steps + j]
    yb_ref[...] = (piece(2) * cu).astype(BF16)

    q_ref[...] = piece(5) * q_scale
    k_ref[...] = piece(6)
    v_ref[...] = piece(7)


def _mixer_in_sample(x, lw, conv_a_state, h0, conv_b_state, *, q_scale, steps):
    n, d = x.shape
    bn = n // steps
    w = lw["w_in"].shape[1] // N_IN_PIECES
    heads, blk, _ = lw["rg_w_a"].shape
    full = lambda shape: pl.BlockSpec(shape, lambda i: (0,) * len(shape))
    rows = lambda dt: jax.ShapeDtypeStruct((n, w), dt)
    return pl.pallas_call(
        functools.partial(_mixer_in_sample_kernel, q_scale=q_scale, steps=steps),
        grid=(1,),
        in_specs=[full((n, d)), _resident((1, d)), _resident((d, N_IN_PIECES * w)),
                  _resident((CONV_A_WIDTH, w)), _resident((1, w)),
                  _resident((heads, blk, blk)), _resident((1, w)), _resident((heads, blk, blk)),
                  _resident((1, w)), _resident((1, w)), _resident((CONV_B_WIDTH, w)),
                  full((CONV_A_WIDTH - 1, bn, w)), full((bn, w)), full((CONV_B_WIDTH - 1, bn, w))],
        out_specs=[full((n, w)), full((n, w)), full((n, w)), full((n, w)), full((n, w)),
                   full((bn, w)), full((CONV_A_WIDTH - 1, bn, w)), full((CONV_B_WIDTH - 1, bn, w))],
        out_shape=[rows(BF16), rows(BF16), rows(F32), rows(F32), rows(F32),
                   jax.ShapeDtypeStruct((bn, w), F32),
                   jax.ShapeDtypeStruct((CONV_A_WIDTH - 1, bn, w), F32),
                   jax.ShapeDtypeStruct((CONV_B_WIDTH - 1, bn, w), F32)],
        compiler_params=_params("arbitrary"),
        name="mixer_in_sample",
    )(x, lw["norm_mix"], lw["w_in"], lw["conv_a_w"], lw["conv_a_b"], lw["rg_w_a"], lw["rg_b_a"],
      lw["rg_w_x"], lw["rg_b_x"], lw["rg_lambda"], lw["conv_b_w"], conv_a_state, h0, conv_b_state)


def _attn_prompt_kernel(bias_ref, q_ref, k_ref, v_ref, cum_ref, o_ref, carry_ref, acc_ref, *, heads):
    tq = q_ref.shape[0]
    dh = q_ref.shape[1] // heads
    qi = pl.program_id(1)
    carry_ref[...] = jnp.zeros(carry_ref.shape, F32)
    acc_ref[...] = jnp.zeros(acc_ref.shape, F32)
    cum_mat = cum_ref[...]
    head_cols = [slice(h * dh, (h + 1) * dh) for h in range(heads)]

    def block(j, valid):
        k0 = pl.multiple_of(j * tq, tq)
        prepared = [_stick_prepare(_dot_nt(q_ref[:, c], k_ref[pl.ds(k0, tq), c]) + bias_ref[h], valid)
                    for h, c in enumerate(head_cols)]
        weights = []
        for h, (t, (older, newer)) in enumerate(prepared):
            sums_new, carry = _stick_sums(carry_ref[h], newer, cum_mat)
            sums_old, carry = _stick_sums(carry, older, cum_mat)
            carry_ref[h] = carry
            weights.append(jnp.exp2(t - jnp.concatenate([sums_old, sums_new], axis=1)).astype(BF16))
        for h, c in enumerate(head_cols):
            acc_ref[h] += _dot(weights[h], v_ref[pl.ds(k0, tq), c])

    diag = lax.broadcasted_iota(jnp.int32, (tq, tq), 1) < lax.broadcasted_iota(jnp.int32, (tq, tq), 0)
    block(qi, diag)

    def body(step, _):
        block(qi - 1 - step, None)
        return 0

    lax.fori_loop(0, qi, body, 0)
    for h, c in enumerate(head_cols):
        o_ref[:, c] = acc_ref[h].astype(o_ref.dtype)


def _attn_prompt(q, k, v, bias, cum_mat, *, heads):
    bsz, t, w = q.shape
    tq = 2 * KV_TILE
    return pl.pallas_call(
        functools.partial(_attn_prompt_kernel, heads=heads),
        grid=(bsz, t // tq),
        in_specs=[pl.BlockSpec(memory_space=pltpu.SMEM),
                  pl.BlockSpec((None, tq, w), lambda b, i: (b, i, 0)),
                  pl.BlockSpec((None, t, w), lambda b, i: (b, 0, 0)),
                  pl.BlockSpec((None, t, w), lambda b, i: (b, 0, 0)),
                  _resident(cum_mat.shape)],
        out_specs=pl.BlockSpec((None, tq, w), lambda b, i: (b, i, 0)),
        out_shape=jax.ShapeDtypeStruct((bsz, t, w), BF16),
        scratch_shapes=[pltpu.VMEM((heads, tq, KV_TILE), F32), pltpu.VMEM((heads, tq, w // heads), F32)],
        compiler_params=_params("parallel", "arbitrary"),
        name="attn_prompt",
    )(bias, q, k, v, cum_mat)


def _attn_sample_kernel(layer_ref, pt_ref, q_ref, kn_ref, vn_ref, *rest, heads, group):
    del layer_ref, pt_ref
    kp_refs, vp_refs = rest[:group], rest[group:2 * group]
    bias_ref, cum_ref, o_ref, knew, vnew, carry_ref, acc_ref = rest[2 * group:]
    dh = q_ref.shape[1]
    rows = heads * SUBLANES
    j = pl.program_id(1)
    cum_mat = cum_ref[...]
    bias = bias_ref[...]
    q_heads = [q_ref[h * SUBLANES:(h + 1) * SUBLANES, :].astype(BF16) for h in range(heads)]

    def logits(k_heads):
        return jnp.concatenate([_dot_nt(q_heads[h], k_heads[h].astype(BF16)) for h in range(heads)], axis=0) + bias

    def weighted_values(weights, v_heads):
        return jnp.concatenate(
            [_dot(weights[h * SUBLANES:(h + 1) * SUBLANES].astype(BF16), v_heads[h].astype(BF16))
             for h in range(heads)], axis=0)

    def page_heads(ref):
        return [ref[pl.ds(h, KV_TILE, stride=heads), :] for h in range(heads)]

    @pl.when((j == 0) & (pl.program_id(0) == 0))
    def _():
        knew[...] = jnp.zeros(knew.shape, F32)
        vnew[...] = jnp.zeros(vnew.shape, F32)

    @pl.when(j == 0)
    def _():
        knew[0:SUBLANES, :] = kn_ref[...]
        vnew[0:SUBLANES, :] = vn_ref[...]
        key = lax.broadcasted_iota(jnp.int32, (rows, KV_TILE), 1)
        step_of_row = lax.broadcasted_iota(jnp.int32, (rows, KV_TILE), 0) % SUBLANES
        t, (operand,) = _stick_prepare(logits([knew[:, h * dh:(h + 1) * dh] for h in range(heads)]),
                                       key < step_of_row)
        sums = _dot(operand, cum_mat)
        carry_ref[...] = sums[:, KV_TILE:]
        acc_ref[...] = weighted_values(jnp.exp2(t - sums[:, :KV_TILE]),
                                       [vnew[:, h * dh:(h + 1) * dh] for h in range(heads)])

    prepared = [_stick_prepare(logits(page_heads(kp_refs[g])), None) for g in range(group)]
    local = [_dot(operand, cum_mat) for _, (operand,) in prepared]
    carry = carry_ref[...]
    weights = []
    for g in range(group):
        weights.append(jnp.exp2(prepared[g][0] - local[g][:, :KV_TILE] - carry))
        carry = carry + local[g][:, KV_TILE:]
    carry_ref[...] = carry
    acc_ref[...] += sum(weighted_values(weights[g], page_heads(vp_refs[g])) for g in range(group))

    @pl.when(j == pl.num_programs(1) - 1)
    def _():
        o_ref[...] = acc_ref[...]


def _attn_sample(layer, page_table, q, k_new, v_new, cache_k, cache_v, bias_rows, cum_mat, *, heads, group):
    bn, rows, dh = q.shape
    w = heads * dh
    n_pages = page_table.shape[0] // bn
    page_rows = cache_k.shape[2]
    assert page_rows == KV_TILE * heads and n_pages % group == 0

    def page_spec(g):
        def index(b, j, layer_ref, pt_ref):
            return layer_ref[0], pt_ref[(b + 1) * n_pages - 1 - (j * group + g)], 0, 0
        return pl.BlockSpec((None, None, page_rows, dh), index)

    per_seq = lambda r, c: pl.BlockSpec((None, r, c), lambda b, j, *_: (b, 0, 0))
    const = lambda shape: pl.BlockSpec(shape, lambda b, j, *_: (0,) * len(shape))
    grid_spec = pltpu.PrefetchScalarGridSpec(
        num_scalar_prefetch=2,
        grid=(bn, n_pages // group),
        in_specs=[per_seq(rows, dh), per_seq(SUBLANES, w), per_seq(SUBLANES, w)]
                 + [page_spec(g) for g in range(group)] * 2
                 + [const((rows, 1)), const(cum_mat.shape)],
        out_specs=per_seq(rows, dh),
        scratch_shapes=[pltpu.VMEM((KV_TILE, w), F32), pltpu.VMEM((KV_TILE, w), F32),
                        pltpu.VMEM((rows, KV_TILE), F32), pltpu.VMEM((rows, dh), F32)],
    )
    return pl.pallas_call(
        functools.partial(_attn_sample_kernel, heads=heads, group=group),
        grid_spec=grid_spec,
        out_shape=jax.ShapeDtypeStruct((bn, rows, dh), F32),
        compiler_params=_params("arbitrary", "arbitrary"),
        name="attn_sample",
    )(layer, page_table, q, k_new, v_new, *([cache_k] * group), *([cache_v] * group), bias_rows, cum_mat)


def _mixer_out_kernel(x_ref, g_ref, ya_ref, yb_ref, yc_ref, wbr_ref, wg_ref, bg_ref, wo_ref, o_ref):
    x = x_ref[...]
    d = x.shape[1]
    xn = _rms(x, g_ref[...]).astype(BF16)
    mixed = jnp.zeros(x.shape, F32)
    for g, y_ref in enumerate((ya_ref, yb_ref, yc_ref)):
        gate = jax.nn.sigmoid(_dot(xn, wg_ref[:, g * d:(g + 1) * d]) + bg_ref[:, g * d:(g + 1) * d])
        mixed = mixed + gate * _dot(y_ref[...], wbr_ref[g])
    o_ref[...] = x + _dot(mixed.astype(BF16), wo_ref[...])


def _mixer_out(x, ya, yb, yc, lw, *, tm):
    n, d = x.shape
    w = ya.shape[1]
    row = lambda c: pl.BlockSpec((tm, c), lambda i: (i, 0))
    return pl.pallas_call(
        _mixer_out_kernel,
        grid=(n // tm,),
        in_specs=[row(d), _resident((1, d)), row(w), row(w), row(w), _resident((N_BRANCH, w, d)),
                  _resident((d, N_BRANCH * d)), _resident((1, N_BRANCH * d)), _resident((d, d))],
        out_specs=row(d),
        out_shape=jax.ShapeDtypeStruct((n, d), F32),
        compiler_params=_params("parallel"),
        name="mixer_out",
    )(x, lw["norm_mix"], ya, yb, yc, lw["w_branch"], lw["w_gate"], lw["b_gate"], lw["w_o"])


def _row_tile(n, want):
    tm = min(n, want)
    assert n % tm == 0
    return tm


def kernel(x_prompt, x_sample, cache_k, cache_v, page_table, state_lru_h, state_conv_a, state_conv_b, norm_ffn1, ffn1_w1, ffn1_w3, ffn1_w2, norm_mix, w_in, conv_a_w, conv_a_b, rg_w_a, rg_b_a, rg_w_x, rg_b_x, rg_lambda, conv_b_w, sb_bias, w_branch, w_gate, b_gate, w_o, norm_ffn2, ffn2_w1, ffn2_w3, ffn2_w2, norm_final):
    depth = w_in.shape[0]
    bp, tp, d = x_prompt.shape
    bn, steps, _ = x_sample.shape
    heads, dh = cache_k.shape[3], cache_k.shape[4]
    w = heads * dh
    n_phys, page = cache_k.shape[1], cache_k.shape[2]
    q_scale = float(dh) ** -0.5 * LOG2_E
    sb_bias = sb_bias * LOG2_E
    cum_mat = _cumsum_matrix()
    pages_per_step = next(g for g in (8, 4, 2, 1) if page_table.shape[1] % g == 0)

    row2 = lambda a: a.reshape(depth, 1, -1)
    weights = dict(
        norm_ffn1=row2(norm_ffn1), ffn1_w1=ffn1_w1.astype(BF16), ffn1_w3=ffn1_w3.astype(BF16),
        ffn1_w2=ffn1_w2.astype(BF16), norm_mix=row2(norm_mix), w_in=w_in.astype(BF16), conv_a_w=conv_a_w,
        conv_a_b=row2(conv_a_b), rg_w_a=rg_w_a.astype(BF16), rg_b_a=row2(rg_b_a), rg_w_x=rg_w_x.astype(BF16),
        rg_b_x=row2(rg_b_x), rg_lambda=row2(rg_lambda), conv_b_w=conv_b_w, w_branch=w_branch.astype(BF16),
        w_gate=w_gate.astype(BF16), b_gate=row2(b_gate), w_o=w_o.astype(BF16), norm_ffn2=row2(norm_ffn2),
        ffn2_w1=ffn2_w1.astype(BF16), ffn2_w3=ffn2_w3.astype(BF16), ffn2_w2=ffn2_w2.astype(BF16))

    cache_k = cache_k.reshape(depth, n_phys, page * heads, dh)
    cache_v = cache_v.reshape(depth, n_phys, page * heads, dh)
    page_flat = page_table.reshape(-1)

    n_p = bp * tp
    n_s = bn * steps
    tm_p = _row_tile(n_p, 512)
    tm_s = _row_tile(n_s, 512)
    tm_mix = _row_tile(tp, 256)

    xp = x_prompt.reshape(n_p, d)
    xs = x_sample.transpose(1, 0, 2).reshape(n_s, d)
    outs = [[] for _ in range(10)]

    def to_seq_major(a):
        return a.reshape(steps, bn, -1).transpose(1, 0, 2)

    for l in range(depth):
        lw = {name: val[l] for name, val in weights.items()}

        xp = _ffn(xp, lw["norm_ffn1"], lw["ffn1_w1"], lw["ffn1_w3"], lw["ffn1_w2"], tm=tm_p)
        ya, yb, q, k, v, kb, vb, h_last, ca_new, cb_new = _mixer_in_prompt(
            xp.reshape(bp, tp, d), lw, tm=tm_mix, q_scale=q_scale)
        yc = _attn_prompt(q, kb, vb, sb_bias[l], cum_mat, heads=heads)
        xp = _mixer_out(xp, ya.reshape(n_p, w), yb.reshape(n_p, w), yc.reshape(n_p, w), lw, tm=tm_p)
        xp = _ffn(xp, lw["norm_ffn2"], lw["ffn2_w1"], lw["ffn2_w3"], lw["ffn2_w2"], tm=tm_p)
        outs[0].append(k.reshape(bp, tp, heads, dh))
        outs[1].append(v.reshape(bp, tp, heads, dh))
        outs[4].append(h_last.reshape(bp, w))
        outs[6].append(ca_new)
        outs[8].append(cb_new)

        xs = _ffn(xs, lw["norm_ffn1"], lw["ffn1_w1"], lw["ffn1_w3"], lw["ffn1_w2"], tm=tm_s)
        ya, yb, q, k, v, h_last, ca_new, cb_new = _mixer_in_sample(
            xs, lw, state_conv_a[l].transpose(1, 0, 2), state_lru_h[l], state_conv_b[l].transpose(1, 0, 2),
            q_scale=q_scale, steps=steps)
        k_sm, v_sm = to_seq_major(k), to_seq_major(v)
        pad_rows = ((0, 0), (0, SUBLANES - steps), (0, 0))
        q_rows = jnp.pad(to_seq_major(q), pad_rows).reshape(bn, SUBLANES, heads, dh).transpose(0, 2, 1, 3)
        bias_rows = jnp.repeat(sb_bias[l], SUBLANES).reshape(heads * SUBLANES, 1)
        yc = _attn_sample(jnp.full((1,), l, jnp.int32), page_flat, q_rows.reshape(bn, heads * SUBLANES, dh),
                          jnp.pad(k_sm, pad_rows), jnp.pad(v_sm, pad_rows), cache_k, cache_v,
                          bias_rows, cum_mat, heads=heads, group=pages_per_step)
        yc = yc.reshape(bn, heads, SUBLANES, dh)[:, :, :steps].transpose(2, 0, 1, 3).reshape(n_s, w).astype(BF16)
        xs = _mixer_out(xs, ya, yb, yc, lw, tm=tm_s)
        xs = _ffn(xs, lw["norm_ffn2"], lw["ffn2_w1"], lw["ffn2_w3"], lw["ffn2_w2"], tm=tm_s)
        outs[2].append(k_sm.reshape(bn, steps, heads, dh))
        outs[3].append(v_sm.reshape(bn, steps, heads, dh))
        outs[5].append(h_last)
        outs[7].append(ca_new.transpose(1, 0, 2))
        outs[9].append(cb_new.transpose(1, 0, 2))

    g_final = norm_final.reshape(1, d)
    y_prompt = _final_norm(xp, g_final, tm=tm_p).reshape(bp, tp, d)
    y_sample = _final_norm(xs, g_final, tm=tm_s).reshape(steps, bn, d).transpose(1, 0, 2)
    return (y_prompt, y_sample) + tuple(jnp.stack(o) for o in outs)
```

```python
import functools

import jax
import jax.numpy as jnp
from jax import lax
from jax.experimental import pallas as pl
from jax.experimental.pallas import tpu as pltpu

F32 = jnp.float32
BF16 = jnp.bfloat16

RMS_EPS = 1e-6
LRU_C = 8.0
CONV_A_WIDTH = 4
CONV_B_WIDTH = 3
N_IN_PIECES = 8
N_BRANCH = 3

V7X_VMEM_LIMIT_BYTES = 56 * 1024 * 1024
SUBLANES = 8
LANES = 128
KV_TILE = 128
NEG_BIG = -1e30
LOG2_E = 1.4426950408889634


def _params(*semantics):
    return pltpu.CompilerParams(dimension_semantics=semantics, vmem_limit_bytes=V7X_VMEM_LIMIT_BYTES)


def _resident(shape):
    zeros = (0,) * len(shape)
    return pl.BlockSpec(shape, lambda *_: zeros, pipeline_mode=pl.Buffered(1))


def _rms(x, g):
    return x * lax.rsqrt(jnp.mean(x * x, axis=-1, keepdims=True) + RMS_EPS) * g


def _dot(a, b):
    return jnp.dot(a, b, preferred_element_type=F32)


def _dot_nt(a, b):
    return lax.dot_general(a, b, (((1,), (1,)), ((), ())), preferred_element_type=F32)


def _softplus(z):
    return jnp.maximum(z, 0.0) + jnp.log(1.0 + jnp.exp(-jnp.abs(z)))


def _cumsum_matrix():
    j = lax.broadcasted_iota(jnp.int32, (2 * KV_TILE, 2 * KV_TILE), 0) % KV_TILE
    s = lax.broadcasted_iota(jnp.int32, (2 * KV_TILE, 2 * KV_TILE), 1)
    return jnp.where((s >= KV_TILE) | (j > s), 1.0, 0.0).astype(BF16)


def _stick_prepare(z, valid):
    neg_abs = lax.bitcast_convert_type(lax.bitcast_convert_type(z, jnp.uint32) | jnp.uint32(0x80000000), F32)
    sp = jnp.maximum(z, 0.0) + jnp.log(1.0 + jnp.exp2(neg_abs)) * LOG2_E
    t = z - sp
    if valid is not None:
        sp = jnp.where(valid, sp, 0.0)
        t = jnp.where(valid, t, NEG_BIG)
    hi = lax.bitcast_convert_type(lax.bitcast_convert_type(sp, jnp.uint32) & jnp.uint32(0xFFFF0000), F32)
    lo = (sp - hi).astype(BF16)
    hi = hi.astype(BF16)
    tiles = range(0, z.shape[1], KV_TILE)
    return t, [jnp.concatenate([hi[:, c:c + KV_TILE], lo[:, c:c + KV_TILE]], axis=1) for c in tiles]


def _stick_sums(carry, sp_operand, cum_mat):
    r = jnp.concatenate([carry, carry], axis=1) + _dot(sp_operand, cum_mat)
    return r[:, :KV_TILE], r[:, KV_TILE:]


def _ffn_kernel(x_ref, g_ref, w1_ref, w3_ref, w2_ref, o_ref, *, f_chunk):
    x = x_ref[...]
    xn = _rms(x, g_ref[...]).astype(BF16)
    acc = jnp.zeros(x.shape, F32)
    for f0 in range(0, w1_ref.shape[1], f_chunk):
        h1 = _dot(xn, w1_ref[:, f0:f0 + f_chunk])
        h3 = _dot(xn, w3_ref[:, f0:f0 + f_chunk])
        act = (h1 * jax.nn.sigmoid(h1) * h3).astype(BF16)
        acc = acc + _dot(act, w2_ref[f0:f0 + f_chunk, :])
    o_ref[...] = x + 0.5 * acc


def _ffn(x, g, w1, w3, w2, *, tm):
    n, d = x.shape
    f = w1.shape[1]
    f_chunk = f // 2 if (f // 2) % LANES == 0 else f
    row = pl.BlockSpec((tm, d), lambda i: (i, 0))
    return pl.pallas_call(
        functools.partial(_ffn_kernel, f_chunk=f_chunk),
        grid=(n // tm,),
        in_specs=[row, _resident((1, d)), _resident((d, f)), _resident((d, f)), _resident((f, d))],
        out_specs=row,
        out_shape=jax.ShapeDtypeStruct((n, d), F32),
        compiler_params=_params("parallel"),
        name="ffn",
    )(x, g, w1, w3, w2)


def _norm_kernel(x_ref, g_ref, o_ref):
    o_ref[...] = _rms(x_ref[...], g_ref[...])


def _final_norm(x, g, *, tm):
    n, d = x.shape
    row = pl.BlockSpec((tm, d), lambda i: (i, 0))
    return pl.pallas_call(
        _norm_kernel, grid=(n // tm,), in_specs=[row, _resident((1, d))], out_specs=row,
        out_shape=jax.ShapeDtypeStruct((n, d), F32), compiler_params=_params("parallel"), name="final_norm",
    )(x, g)


def _lru_gates(xc, wa_ref, ba, wx_ref, bx, lam):
    heads, blk, _ = wa_ref.shape
    xcb = xc.astype(BF16)
    r_pre = jnp.concatenate([_dot(xcb[:, h * blk:(h + 1) * blk], wa_ref[h]) for h in range(heads)], axis=1)
    i_pre = jnp.concatenate([_dot(xcb[:, h * blk:(h + 1) * blk], wx_ref[h]) for h in range(heads)], axis=1)
    r = jax.nn.sigmoid(r_pre + ba)
    i = jax.nn.sigmoid(i_pre + bx)
    log_a = (-LRU_C * _softplus(-lam)) * r
    a = jnp.exp(log_a)
    mult = jnp.sqrt(-jnp.tanh(log_a) * (a * a + 1.0))
    return a, mult * (i * xc)


def _mixer_in_prompt_kernel(x_ref, g_ref, w_in_ref, caw_ref, cab_ref, wa_ref, ba_ref, wx_ref, bx_ref, lam_ref,
                            cbw_ref,
                            ya_ref, yb_ref, q_ref, k_ref, v_ref, kb_ref, vb_ref, h_ref, ca_ref, cb_ref,
                            ax_ext, u_ext, hseq, h_carry, *, q_scale):
    tm = x_ref.shape[0]
    w = ya_ref.shape[1]
    pad = SUBLANES

    @pl.when(pl.program_id(1) == 0)
    def _():
        ax_ext[0:pad, :] = jnp.zeros((pad, w), F32)
        u_ext[0:pad, :] = jnp.zeros((pad, w), F32)
        h_carry[...] = jnp.zeros(h_carry.shape, F32)

    xn = _rms(x_ref[...], g_ref[...]).astype(BF16)

    def piece(p):
        return _dot(xn, w_in_ref[:, p * w:(p + 1) * w])

    ax_ext[pad:pad + tm, :] = piece(0)
    xc = cab_ref[...]
    ext = ax_ext[...]
    for j in range(CONV_A_WIDTH):
        delay = CONV_A_WIDTH - 1 - j
        tap = pltpu.roll(ext, delay, 0) if delay else ext
        xc = xc + tap[pad:, :] * caw_ref[j:j + 1, :]
    ca_ref[...] = ax_ext[pad + tm - (CONV_A_WIDTH - 1):pad + tm, :]
    ax_ext[0:pad, :] = ax_ext[tm:tm + pad, :]

    a, b = _lru_gates(xc, wa_ref, ba_ref[...], wx_ref, bx_ref[...], lam_ref[...])

    q_ref[...] = (piece(5) * q_scale).astype(BF16)
    k = piece(6)
    k_ref[...] = k
    kb_ref[...] = k.astype(BF16)
    v = piece(7)
    v_ref[...] = v
    vb_ref[...] = v.astype(BF16)
    u_ext[pad:pad + tm, :] = piece(3) * piece(4)
    sc_b = piece(2)
    a_gate = piece(1)

    groups = tm // SUBLANES
    a = a.reshape(groups, SUBLANES, w)
    b = b.reshape(groups, SUBLANES, w)
    row = lax.broadcasted_iota(jnp.int32, (groups, SUBLANES, w), 1)
    for k in (1, 2, 4):
        keep = row >= k
        b = jnp.where(keep, b + a * pltpu.roll(b, k, 1), b)
        a = jnp.where(keep, a * pltpu.roll(a, k, 1), a)
    h = h_carry[0:1, :]
    for r in range(groups):
        blk = b[r] + a[r] * h
        hseq[r * SUBLANES:(r + 1) * SUBLANES, :] = blk
        h = blk[SUBLANES - 1:SUBLANES]
    h_carry[0:1, :] = h
    h_ref[...] = h
    ya_ref[...] = (hseq[...] * jax.nn.gelu(a_gate)).astype(BF16)

    cu = jnp.zeros((tm, w), F32)
    ext = u_ext[...]
    for j in range(CONV_B_WIDTH):
        delay = CONV_B_WIDTH - 1 - j
        tap = pltpu.roll(ext, delay, 0) if delay else ext
        cu = cu + tap[pad:, :] * cbw_ref[j:j + 1, :]
    cb_ref[...] = u_ext[pad + tm - (CONV_B_WIDTH - 1):pad + tm, :]
    u_ext[0:pad, :] = u_ext[tm:tm + pad, :]
    yb_ref[...] = (sc_b * cu).astype(BF16)


def _mixer_in_prompt(x, lw, *, tm, q_scale):
    bsz, t, d = x.shape
    w = lw["w_in"].shape[1] // N_IN_PIECES
    heads, blk, _ = lw["rg_w_a"].shape
    tile = lambda: pl.BlockSpec((None, tm, w), lambda b, i: (b, i, 0))
    state = lambda rows: pl.BlockSpec((None, rows, w), lambda b, i: (b, 0, 0))
    seq = lambda dt: jax.ShapeDtypeStruct((bsz, t, w), dt)
    return pl.pallas_call(
        functools.partial(_mixer_in_prompt_kernel, q_scale=q_scale),
        grid=(bsz, t // tm),
        in_specs=[pl.BlockSpec((None, tm, d), lambda b, i: (b, i, 0)), _resident((1, d)),
                  _resident((d, N_IN_PIECES * w)), _resident((CONV_A_WIDTH, w)), _resident((1, w)),
                  _resident((heads, blk, blk)), _resident((1, w)), _resident((heads, blk, blk)),
                  _resident((1, w)), _resident((1, w)), _resident((CONV_B_WIDTH, w))],
        out_specs=[tile(), tile(), tile(), tile(), tile(), tile(), tile(),
                   state(1), state(CONV_A_WIDTH - 1), state(CONV_B_WIDTH - 1)],
        out_shape=[seq(BF16), seq(BF16), seq(BF16), seq(F32), seq(F32), seq(BF16), seq(BF16),
                   jax.ShapeDtypeStruct((bsz, 1, w), F32),
                   jax.ShapeDtypeStruct((bsz, CONV_A_WIDTH - 1, w), F32),
                   jax.ShapeDtypeStruct((bsz, CONV_B_WIDTH - 1, w), F32)],
        scratch_shapes=[pltpu.VMEM((tm + SUBLANES, w), F32), pltpu.VMEM((tm + SUBLANES, w), F32),
                        pltpu.VMEM((tm, w), F32), pltpu.VMEM((SUBLANES, w), F32)],
        compiler_params=_params("parallel", "arbitrary"),
        name="mixer_in_prompt",
    )(x, lw["norm_mix"], lw["w_in"], lw["conv_a_w"], lw["conv_a_b"], lw["rg_w_a"], lw["rg_b_a"],
      lw["rg_w_x"], lw["rg_b_x"], lw["rg_lambda"], lw["conv_b_w"])


def _mixer_in_sample_kernel(x_ref, g_ref, w_in_ref, caw_ref, cab_ref, wa_ref, ba_ref, wx_ref, bx_ref, lam_ref,
                            cbw_ref, sa_ref, h0_ref, sb_ref,
                            ya_ref, yb_ref, q_ref, k_ref, v_ref, h_ref, ca_ref, cb_ref, *, q_scale, steps):
    n = x_ref.shape[0]
    bn = n // steps
    w = ya_ref.shape[1]
    xn = _rms(x_ref[...], g_ref[...]).astype(BF16)

    def piece(p):
        return _dot(xn, w_in_ref[:, p * w:(p + 1) * w])

    def slabs(u):
        return [u[s * bn:(s + 1) * bn] for s in range(steps)]

    def conv(history, weights_ref, width):
        out = []
        for s in range(steps):
            y = history[s] * weights_ref[0:1, :]
            for j in range(1, width):
                y = y + history[s + j] * weights_ref[j:j + 1, :]
            out.append(y)
        return jnp.concatenate(out, axis=0)

    hist_a = [sa_ref[j] for j in range(CONV_A_WIDTH - 1)] + slabs(piece(0))
    xc = conv(hist_a, caw_ref, CONV_A_WIDTH) + cab_ref[...]
    for j in range(CONV_A_WIDTH - 1):
        ca_ref[j] = hist_a[steps + j]
    a, b = _lru_gates(xc, wa_ref, ba_ref[...], wx_ref, bx_ref[...], lam_ref[...])
    h = h0_ref[...]
    hs = []
    for s in range(steps):
        h = a[s * bn:(s + 1) * bn] * h + b[s * bn:(s + 1) * bn]
        hs.append(h)
    h_ref[...] = h
    ya_ref[...] = (jnp.concatenate(hs, axis=0) * jax.nn.gelu(piece(1))).astype(BF16)

    hist_b = [sb_ref[j] for j in range(CONV_B_WIDTH - 1)] + slabs(piece(3) * piece(4))
    cu = conv(hist_b, cbw_ref, CONV_B_WIDTH)
    for j in range(CONV_B_WIDTH - 1):
        cb_ref[j] = hist_b[steps + j]
    yb_ref[...] = (piece(2) * cu).astype(BF16)

    q_ref[...] = piece(5) * q_scale
    k_ref[...] = piece(6)
    v_ref[...] = piece(7)


def _mixer_in_sample(x, lw, conv_a_state, h0, conv_b_state, *, q_scale, steps):
    n, d = x.shape
    bn = n // steps
    w = lw["w_in"].shape[1] // N_IN_PIECES
    heads, blk, _ = lw["rg_w_a"].shape
    full = lambda shape: pl.BlockSpec(shape, lambda i: (0,) * len(shape))
    rows = lambda dt: jax.ShapeDtypeStruct((n, w), dt)
    return pl.pallas_call(
        functools.partial(_mixer_in_sample_kernel, q_scale=q_scale, steps=steps),
        grid=(1,),
        in_specs=[full((n, d)), _resident((1, d)), _resident((d, N_IN_PIECES * w)),
                  _resident((CONV_A_WIDTH, w)), _resident((1, w)),
                  _resident((heads, blk, blk)), _resident((1, w)), _resident((heads, blk, blk)),
                  _resident((1, w)), _resident((1, w)), _resident((CONV_B_WIDTH, w)),
                  full((CONV_A_WIDTH - 1, bn, w)), full((bn, w)), full((CONV_B_WIDTH - 1, bn, w))],
        out_specs=[full((n, w)), full((n, w)), full((n, w)), full((n, w)), full((n, w)),
                   full((bn, w)), full((CONV_A_WIDTH - 1, bn, w)), full((CONV_B_WIDTH - 1, bn, w))],
        out_shape=[rows(BF16), rows(BF16), rows(F32), rows(F32), rows(F32),
                   jax.ShapeDtypeStruct((bn, w), F32),
                   jax.ShapeDtypeStruct((CONV_A_WIDTH - 1, bn, w), F32),
                   jax.ShapeDtypeStruct((CONV_B_WIDTH - 1, bn, w), F32)],
        compiler_params=_params("arbitrary"),
        name="mixer_in_sample",
    )(x, lw["norm_mix"], lw["w_in"], lw["conv_a_w"], lw["conv_a_b"], lw["rg_w_a"], lw["rg_b_a"],
      lw["rg_w_x"], lw["rg_b_x"], lw["rg_lambda"], lw["conv_b_w"], conv_a_state, h0, conv_b_state)


def _attn_prompt_kernel(bias_ref, q_ref, k_ref, v_ref, cum_ref, o_ref, carry_ref, acc_ref, *, heads):
    tq = q_ref.shape[0]
    dh = q_ref.shape[1] // heads
    qi = pl.program_id(1)
    carry_ref[...] = jnp.zeros(carry_ref.shape, F32)
    acc_ref[...] = jnp.zeros(acc_ref.shape, F32)
    cum_mat = cum_ref[...]
    head_cols = [slice(h * dh, (h + 1) * dh) for h in range(heads)]

    def block(j, valid):
        k0 = pl.multiple_of(j * tq, tq)
        prepared = [_stick_prepare(_dot_nt(q_ref[:, c], k_ref[pl.ds(k0, tq), c]) + bias_ref[h], valid)
                    for h, c in enumerate(head_cols)]
        weights = []
        for h, (t, (older, newer)) in enumerate(prepared):
            sums_new, carry = _stick_sums(carry_ref[h], newer, cum_mat)
            sums_old, carry = _stick_sums(carry, older, cum_mat)
            carry_ref[h] = carry
            weights.append(jnp.exp2(t - jnp.concatenate([sums_old, sums_new], axis=1)).astype(BF16))
        for h, c in enumerate(head_cols):
            acc_ref[h] += _dot(weights[h], v_ref[pl.ds(k0, tq), c])

    diag = lax.broadcasted_iota(jnp.int32, (tq, tq), 1) < lax.broadcasted_iota(jnp.int32, (tq, tq), 0)
    block(qi, diag)

    def body(step, _):
        block(qi - 1 - step, None)
        return 0

    lax.fori_loop(0, qi, body, 0)
    for h, c in enumerate(head_cols):
        o_ref[:, c] = acc_ref[h].astype(o_ref.dtype)


def _attn_prompt(q, k, v, bias, cum_mat, *, heads):
    bsz, t, w = q.shape
    tq = 2 * KV_TILE
    return pl.pallas_call(
        functools.partial(_attn_prompt_kernel, heads=heads),
        grid=(bsz, t // tq),
        in_specs=[pl.BlockSpec(memory_space=pltpu.SMEM),
                  pl.BlockSpec((None, tq, w), lambda b, i: (b, i, 0)),
                  pl.BlockSpec((None, t, w), lambda b, i: (b, 0, 0)),
                  pl.BlockSpec((None, t, w), lambda b, i: (b, 0, 0)),
                  _resident(cum_mat.shape)],
        out_specs=pl.BlockSpec((None, tq, w), lambda b, i: (b, i, 0)),
        out_shape=jax.ShapeDtypeStruct((bsz, t, w), BF16),
        scratch_shapes=[pltpu.VMEM((heads, tq, KV_TILE), F32), pltpu.VMEM((heads, tq, w // heads), F32)],
        compiler_params=_params("parallel", "arbitrary"),
        name="attn_prompt",
    )(bias, q, k, v, cum_mat)


def _attn_sample_kernel(layer_ref, pt_ref, q_ref, kn_ref, vn_ref, *rest, heads, group):
    del layer_ref, pt_ref
    kp_refs, vp_refs = rest[:group], rest[group:2 * group]
    bias_ref, cum_ref, o_ref, knew, vnew, carry_ref, acc_ref = rest[2 * group:]
    dh = q_ref.shape[1]
    rows = heads * SUBLANES
    j = pl.program_id(1)
    cum_mat = cum_ref[...]
    bias = bias_ref[...]
    q = q_ref[...]
    head_of_row = lax.broadcasted_iota(jnp.int32, (rows, dh), 0) // SUBLANES
    q_diag = jnp.concatenate([jnp.where(head_of_row == h, q, 0.0) for h in range(heads)], axis=1).astype(BF16)

    def heads_on_lanes(per_head):
        return jnp.concatenate([x.astype(BF16) for x in per_head], axis=1)

    def logits(k_heads):
        return _dot_nt(q_diag, heads_on_lanes(k_heads)) + bias

    def weighted_values(weights, v_heads):
        full = _dot(weights.astype(BF16), heads_on_lanes(v_heads))
        return jnp.concatenate([full[h * SUBLANES:(h + 1) * SUBLANES, h * dh:(h + 1) * dh] for h in range(heads)],
                               axis=0)

    def page_heads(ref):
        return [ref[pl.ds(h, KV_TILE, stride=heads), :] for h in range(heads)]

    @pl.when((j == 0) & (pl.program_id(0) == 0))
    def _():
        knew[...] = jnp.zeros(knew.shape, F32)
        vnew[...] = jnp.zeros(vnew.shape, F32)

    @pl.when(j == 0)
    def _():
        knew[0:SUBLANES, :] = kn_ref[...]
        vnew[0:SUBLANES, :] = vn_ref[...]
        key = lax.broadcasted_iota(jnp.int32, (rows, KV_TILE), 1)
        step_of_row = lax.broadcasted_iota(jnp.int32, (rows, KV_TILE), 0) % SUBLANES
        t, (operand,) = _stick_prepare(logits([knew[:, h * dh:(h + 1) * dh] for h in range(heads)]),
                                       key < step_of_row)
        sums = _dot(operand, cum_mat)
        carry_ref[...] = sums[:, KV_TILE:]
        acc_ref[...] = weighted_values(jnp.exp2(t - sums[:, :KV_TILE]),
                                       [vnew[:, h * dh:(h + 1) * dh] for h in range(heads)])

    prepared = [_stick_prepare(logits(page_heads(kp_refs[g])), None) for g in range(group)]
    local = [_dot(operand, cum_mat) for _, (operand,) in prepared]
    carry = carry_ref[...]
    weights = []
    for g in range(group):
        weights.append(jnp.exp2(prepared[g][0] - local[g][:, :KV_TILE] - carry))
        carry = carry + local[g][:, KV_TILE:]
    carry_ref[...] = carry
    acc_ref[...] += sum(weighted_values(weights[g], page_heads(vp_refs[g])) for g in range(group))

    @pl.when(j == pl.num_programs(1) - 1)
    def _():
        o_ref[...] = acc_ref[...]


def _attn_sample(layer, page_table, q, k_new, v_new, cache_k, cache_v, bias_rows, cum_mat, *, heads, group):
    bn, rows, dh = q.shape
    w = heads * dh
    n_pages = page_table.shape[0] // bn
    page_rows = cache_k.shape[2]
    assert page_rows == KV_TILE * heads and n_pages % group == 0

    def page_spec(g):
        def index(b, j, layer_ref, pt_ref):
            return layer_ref[0], pt_ref[(b + 1) * n_pages - 1 - (j * group + g)], 0, 0
        return pl.BlockSpec((None, None, page_rows, dh), index)

    per_seq = lambda r, c: pl.BlockSpec((None, r, c), lambda b, j, *_: (b, 0, 0))
    const = lambda shape: pl.BlockSpec(shape, lambda b, j, *_: (0,) * len(shape))
    grid_spec = pltpu.PrefetchScalarGridSpec(
        num_scalar_prefetch=2,
        grid=(bn, n_pages // group),
        in_specs=[per_seq(rows, dh), per_seq(SUBLANES, w), per_seq(SUBLANES, w)]
                 + [page_spec(g) for g in range(group)] * 2
                 + [const((rows, 1)), const(cum_mat.shape)],
        out_specs=per_seq(rows, dh),
        scratch_shapes=[pltpu.VMEM((KV_TILE, w), F32), pltpu.VMEM((KV_TILE, w), F32),
                        pltpu.VMEM((rows, KV_TILE), F32), pltpu.VMEM((rows, dh), F32)],
    )
    return pl.pallas_call(
        functools.partial(_attn_sample_kernel, heads=heads, group=group),
        grid_spec=grid_spec,
        out_shape=jax.ShapeDtypeStruct((bn, rows, dh), F32),
        compiler_params=_params("arbitrary", "arbitrary"),
        name="attn_sample",
    )(layer, page_table, q, k_new, v_new, *([cache_k] * group), *([cache_v] * group), bias_rows, cum_mat)


def _mixer_out_kernel(x_ref, g_ref, ya_ref, yb_ref, yc_ref, wbr_ref, wg_ref, bg_ref, wo_ref, o_ref):
    x = x_ref[...]
    d = x.shape[1]
    xn = _rms(x, g_ref[...]).astype(BF16)
    mixed = jnp.zeros(x.shape, F32)
    for g, y_ref in enumerate((ya_ref, yb_ref, yc_ref)):
        gate = jax.nn.sigmoid(_dot(xn, wg_ref[:, g * d:(g + 1) * d]) + bg_ref[:, g * d:(g + 1) * d])
        mixed = mixed + gate * _dot(y_ref[...], wbr_ref[g])
    o_ref[...] = x + _dot(mixed.astype(BF16), wo_ref[...])


def _mixer_out(x, ya, yb, yc, lw, *, tm):
    n, d = x.shape
    w = ya.shape[1]
    row = lambda c: pl.BlockSpec((tm, c), lambda i: (i, 0))
    return pl.pallas_call(
        _mixer_out_kernel,
        grid=(n // tm,),
        in_specs=[row(d), _resident((1, d)), row(w), row(w), row(w), _resident((N_BRANCH, w, d)),
                  _resident((d, N_BRANCH * d)), _resident((1, N_BRANCH * d)), _resident((d, d))],
        out_specs=row(d),
        out_shape=jax.ShapeDtypeStruct((n, d), F32),
        compiler_params=_params("parallel"),
        name="mixer_out",
    )(x, lw["norm_mix"], ya, yb, yc, lw["w_branch"], lw["w_gate"], lw["b_gate"], lw["w_o"])


def _row_tile(n, want):
    tm = min(n, want)
    assert n % tm == 0
    return tm


def kernel(x_prompt, x_sample, cache_k, cache_v, page_table, state_lru_h, state_conv_a, state_conv_b, norm_ffn1, ffn1_w1, ffn1_w3, ffn1_w2, norm_mix, w_in, conv_a_w, conv_a_b, rg_w_a, rg_b_a, rg_w_x, rg_b_x, rg_lambda, conv_b_w, sb_bias, w_branch, w_gate, b_gate, w_o, norm_ffn2, ffn2_w1, ffn2_w3, ffn2_w2, norm_final):
    depth = w_in.shape[0]
    bp, tp, d = x_prompt.shape
    bn, steps, _ = x_sample.shape
    heads, dh = cache_k.shape[3], cache_k.shape[4]
    w = heads * dh
    n_phys, page = cache_k.shape[1], cache_k.shape[2]
    q_scale = float(dh) ** -0.5 * LOG2_E
    sb_bias = sb_bias * LOG2_E
    cum_mat = _cumsum_matrix()
    pages_per_step = next(g for g in (8, 4, 2, 1) if page_table.shape[1] % g == 0)

    row2 = lambda a: a.reshape(depth, 1, -1)
    weights = dict(
        norm_ffn1=row2(norm_ffn1), ffn1_w1=ffn1_w1.astype(BF16), ffn1_w3=ffn1_w3.astype(BF16),
        ffn1_w2=ffn1_w2.astype(BF16), norm_mix=row2(norm_mix), w_in=w_in.astype(BF16), conv_a_w=conv_a_w,
        conv_a_b=row2(conv_a_b), rg_w_a=rg_w_a.astype(BF16), rg_b_a=row2(rg_b_a), rg_w_x=rg_w_x.astype(BF16),
        rg_b_x=row2(rg_b_x), rg_lambda=row2(rg_lambda), conv_b_w=conv_b_w, w_branch=w_branch.astype(BF16),
        w_gate=w_gate.astype(BF16), b_gate=row2(b_gate), w_o=w_o.astype(BF16), norm_ffn2=row2(norm_ffn2),
        ffn2_w1=ffn2_w1.astype(BF16), ffn2_w3=ffn2_w3.astype(BF16), ffn2_w2=ffn2_w2.astype(BF16))

    cache_k = cache_k.reshape(depth, n_phys, page * heads, dh)
    cache_v = cache_v.reshape(depth, n_phys, page * heads, dh)
    page_flat = page_table.reshape(-1)

    n_p = bp * tp
    n_s = bn * steps
    tm_p = _row_tile(n_p, 512)
    tm_s = _row_tile(n_s, 512)
    tm_mix = _row_tile(tp, 256)

    xp = x_prompt.reshape(n_p, d)
    xs = x_sample.transpose(1, 0, 2).reshape(n_s, d)
    outs = [[] for _ in range(10)]

    def to_seq_major(a):
        return a.reshape(steps, bn, -1).transpose(1, 0, 2)

    for l in range(depth):
        lw = {name: val[l] for name, val in weights.items()}

        xp = _ffn(xp, lw["norm_ffn1"], lw["ffn1_w1"], lw["ffn1_w3"], lw["ffn1_w2"], tm=tm_p)
        ya, yb, q, k, v, kb, vb, h_last, ca_new, cb_new = _mixer_in_prompt(
            xp.reshape(bp, tp, d), lw, tm=tm_mix, q_scale=q_scale)
        yc = _attn_prompt(q, kb, vb, sb_bias[l], cum_mat, heads=heads)
        xp = _mixer_out(xp, ya.reshape(n_p, w), yb.reshape(n_p, w), yc.reshape(n_p, w), lw, tm=tm_p)
        xp = _ffn(xp, lw["norm_ffn2"], lw["ffn2_w1"], lw["ffn2_w3"], lw["ffn2_w2"], tm=tm_p)
        outs[0].append(k.reshape(bp, tp, heads, dh))
        outs[1].append(v.reshape(bp, tp, heads, dh))
        outs[4].append(h_last.reshape(bp, w))
        outs[6].append(ca_new)
        outs[8].append(cb_new)

        xs = _ffn(xs, lw["norm_ffn1"], lw["ffn1_w1"], lw["ffn1_w3"], lw["ffn1_w2"], tm=tm_s)
        ya, yb, q, k, v, h_last, ca_new, cb_new = _mixer_in_sample(
            xs, lw, state_conv_a[l].transpose(1, 0, 2), state_lru_h[l], state_conv_b[l].transpose(1, 0, 2),
            q_scale=q_scale, steps=steps)
        k_sm, v_sm = to_seq_major(k), to_seq_major(v)
        pad_rows = ((0, 0), (0, SUBLANES - steps), (0, 0))
        q_rows = jnp.pad(to_seq_major(q), pad_rows).reshape(bn, SUBLANES, heads, dh).transpose(0, 2, 1, 3)
        bias_rows = jnp.repeat(sb_bias[l], SUBLANES).reshape(heads * SUBLANES, 1)
        yc = _attn_sample(jnp.full((1,), l, jnp.int32), page_flat, q_rows.reshape(bn, heads * SUBLANES, dh),
                          jnp.pad(k_sm, pad_rows), jnp.pad(v_sm, pad_rows), cache_k, cache_v,
                          bias_rows, cum_mat, heads=heads, group=pages_per_step)
        yc = yc.reshape(bn, heads, SUBLANES, dh)[:, :, :steps].transpose(2, 0, 1, 3).reshape(n_s, w).astype(BF16)
        xs = _mixer_out(xs, ya, yb, yc, lw, tm=tm_s)
        xs = _ffn(xs, lw["norm_ffn2"], lw["ffn2_w1"], lw["ffn2_w3"], lw["ffn2_w2"], tm=tm_s)
        outs[2].append(k_sm.reshape(bn, steps, heads, dh))
        outs[3].append(v_sm.reshape(bn, steps, heads, dh))
        outs[5].append(h_last)
        outs[7].append(ca_new.transpose(1, 0, 2))
        outs[9].append(cb_new.transpose(1, 0, 2))

    g_final = norm_final.reshape(1, d)
    y_prompt = _final_norm(xp, g_final, tm=tm_p).reshape(bp, tp, d)
    y_sample = _final_norm(xs, g_final, tm=tm_s).reshape(steps, bn, d).transpose(1, 0, 2)
    return (y_prompt, y_sample) + tuple(jnp.stack(o) for o in outs)
```
